```python
import math
import jax
import jax.numpy as jnp
from jax import lax
import numpy as np

D_MODEL = 2048
BATCH = 4
SEQ = 2048
DEPTH = 4
DEC_BATCH = 128
DEC_SEQ = 8
PAST_LEN = 8192
PAGE_SIZE = 128

F32 = jnp.float32
EPS = 1e-6
Q_BLOCK = 128
N_ATT_LAYERS = (DEPTH + 1) // 2
N_SSM_LAYERS = DEPTH // 2
SB_HEADS = 8
SB_KV_HEADS = 2
SB_HEAD_DIM = 128
SB_REP = SB_HEADS // SB_KV_HEADS
SB_SCALE = SB_HEAD_DIM ** -0.5
MLA_HEADS = 8
MLA_Q_LORA = 768
MLA_KV_LORA = 512
MLA_NOPE = 128
MLA_ROPE = 64
MLA_V = 128
MLA_SCALE = (MLA_NOPE + MLA_ROPE) ** -0.5
ROPE_BASE = 10000.0
ATT_IN_SIZES = (SB_HEADS * SB_HEAD_DIM, SB_KV_HEADS * SB_HEAD_DIM, SB_KV_HEADS * SB_HEAD_DIM,
                MLA_Q_LORA, MLA_KV_LORA, MLA_ROPE)
ATT_IN = sum(ATT_IN_SIZES)
ATT_MIX = SB_HEADS * SB_HEAD_DIM + MLA_HEADS * MLA_V
SSM_WIDTH = D_MODEL
SSM_GROUP = 16
SSM_GROUPS = SSM_WIDTH // SSM_GROUP
SSM_STATE = 64
FFN_HIDDEN = 256 * ((8 * D_MODEL // 3 + 255) // 256)
CONV_W = 3

kernel_name = 'hybrid_stickbreak_mla_s5_convffn_step'


def rmsnorm(x, g):
    xf = x.astype(F32)
    y = xf * lax.rsqrt(jnp.mean(xf * xf, axis=-1, keepdims=True) + EPS)
    return (y * g.astype(F32)).astype(x.dtype)


def rope(x, pos):
    half = x.shape[-1] // 2
    inv = ROPE_BASE ** (-jnp.arange(half, dtype=F32) / half)
    ang = pos.astype(F32)[:, None] * inv[None, :]
    shape = (1, pos.shape[0]) + (1,) * (x.ndim - 3) + (half,)
    cos = jnp.cos(ang).reshape(shape)
    sin = jnp.sin(ang).reshape(shape)
    x1 = x[..., :half].astype(F32)
    x2 = x[..., half:].astype(F32)
    return jnp.concatenate([x1 * cos - x2 * sin, x1 * sin + x2 * cos], axis=-1).astype(x.dtype)


def _split_cols(a, sizes):
    out, off = [], 0
    for n in sizes:
        out.append(a[..., off:off + n])
        off += n
    return out


def _segment_weighted_sum(w, vals, spec):
    out, off = None, 0
    for v in vals:
        n = v.shape[1]
        term = jnp.einsum(spec, w[..., off:off + n], v)
        out = term if out is None else out + term
        off += n
    return out


def sweep_query_blocks(attend, q_arrays, q_pos):
    tq = q_pos.shape[0]
    if tq <= Q_BLOCK or tq % Q_BLOCK != 0:
        return attend(*q_arrays, q_pos)
    nb = tq // Q_BLOCK
    blocks = tuple(a.reshape(a.shape[0], nb, Q_BLOCK, *a.shape[2:]).swapaxes(0, 1) for a in q_arrays)
    out = lax.map(lambda args: attend(*args[0], args[1]), (blocks, q_pos.reshape(nb, Q_BLOCK)))
    out = out.swapaxes(0, 1)
    return out.reshape(out.shape[0], tq, *out.shape[3:])


def stick_breaking_attend(q, q_pos, ks, vs, k_pos):
    z = jnp.concatenate([jnp.einsum('bqgrd,bkgd->bgrqk', q, k) for k in ks], axis=-1).astype(F32) * SB_SCALE
    visible = k_pos[None, :] < q_pos[:, None]
    log_stay = jnp.where(visible, jax.nn.log_sigmoid(-z), 0.0)
    log_later = lax.cumsum(log_stay, axis=z.ndim - 1, reverse=True) - log_stay
    w = jnp.where(visible, jnp.exp(jax.nn.log_sigmoid(z) + log_later), 0.0).astype(q.dtype)
    return _segment_weighted_sum(w, vs, 'bgrqk,bkgd->bqgrd')


def mla_attend(q_abs, q_pe, q_pos, cs, rs, k_pos):
    s = jnp.concatenate([jnp.einsum('bqhc,bkc->bhqk', q_abs, c) + jnp.einsum('bqhr,bkr->bhqk', q_pe, r)
                         for c, r in zip(cs, rs)], axis=-1).astype(F32) * MLA_SCALE
    visible = k_pos[None, :] <= q_pos[:, None]
    p = jax.nn.softmax(jnp.where(visible, s, -jnp.inf), axis=-1).astype(q_abs.dtype)
    return _segment_weighted_sum(p, cs, 'bhqk,bkc->bqhc')


def attn_mixer(h, q_pos, k_pos, past, w_in, g_q, g_kv, w_uq, w_ukv, w_out):
    b, t, _ = h.shape
    q_sb, k_sb, v_sb, c_q, c_kv, k_pe = _split_cols(h @ w_in, ATT_IN_SIZES)
    q_sb = q_sb.reshape(b, t, SB_KV_HEADS, SB_REP, SB_HEAD_DIM)
    k_sb = k_sb.reshape(b, t, SB_KV_HEADS, SB_HEAD_DIM)
    v_sb = v_sb.reshape(b, t, SB_KV_HEADS, SB_HEAD_DIM)
    q = (rmsnorm(c_q, g_q) @ w_uq).reshape(b, t, MLA_HEADS, MLA_NOPE + MLA_ROPE)
    q_nope = q[..., :MLA_NOPE]
    q_pe = rope(q[..., MLA_NOPE:], q_pos)
    c_kv = rmsnorm(c_kv, g_kv)
    k_pe = rope(k_pe, q_pos)
    w_uk = w_ukv[..., :MLA_NOPE]
    w_uv = w_ukv[..., MLA_NOPE:]
    q_abs = jnp.einsum('bthd,chd->bthc', q_nope, w_uk)
    new_rows = (k_sb, v_sb, c_kv, k_pe)
    segs = [new_rows] if past is None else [past, new_rows]
    ks = [s[0] for s in segs]
    vs = [s[1] for s in segs]
    cs = [s[2] for s in segs]
    rs = [s[3] for s in segs]
    o_sb = sweep_query_blocks(lambda qb, qp: stick_breaking_attend(qb, qp, ks, vs, k_pos), (q_sb,), q_pos)
    o_lat = sweep_query_blocks(lambda qa, qr, qp: mla_attend(qa, qr, qp, cs, rs, k_pos), (q_abs, q_pe), q_pos)
    o_mla = jnp.einsum('bthc,chd->bthd', o_lat, w_uv)
    o = jnp.concatenate([o_sb.reshape(b, t, -1), o_mla.reshape(b, t, -1)], axis=-1) @ w_out
    return o.astype(h.dtype), new_rows


def _zoh(a_re, a_im, log_step, b_re, b_im):
    dt = jnp.exp(log_step.astype(F32))[:, None]
    ar = a_re.astype(F32)
    ai = a_im.astype(F32)
    mag = jnp.exp(ar * dt)
    lam_r = mag * jnp.cos(ai * dt)
    lam_i = mag * jnp.sin(ai * dt)
    den = ar * ar + ai * ai
    nr = lam_r - 1.0
    fr = (nr * ar + lam_i * ai) / den
    fi = (lam_i * ar - nr * ai) / den
    br = b_re.astype(F32)
    bi = b_im.astype(F32)
    bbar_r = fr[..., None] * br - fi[..., None] * bi
    bbar_i = fr[..., None] * bi + fi[..., None] * br
    return lam_r, lam_i, bbar_r, bbar_i


def _complex_affine_combine(e1, e2):
    a1r, a1i, b1r, b1i = e1
    a2r, a2i, b2r, b2i = e2
    return (a2r * a1r - a2i * a1i, a2r * a1i + a2i * a1r,
            a2r * b1r - a2i * b1i + b2r, a2r * b1i + a2i * b1r + b2i)


def ssm_mixer(h, h0, w_in, a_re, a_im, log_step, b_re, b_im, c_re, c_im, d_skip, w_glu, w_out):
    b, t, _ = h.shape
    u = (h @ w_in).astype(F32)
    ug = u.reshape(b, t, SSM_GROUPS, SSM_GROUP)
    lam_r, lam_i, bbar_r, bbar_i = _zoh(a_re, a_im, log_step, b_re, b_im)
    xr = jnp.einsum('gph,btgh->btgp', bbar_r, ug)
    xi = jnp.einsum('gph,btgh->btgp', bbar_i, ug)
    if h0 is not None:
        h0r = h0[0].astype(F32)
        h0i = h0[1].astype(F32)
        xr = xr.at[:, 0].add(lam_r * h0r - lam_i * h0i)
        xi = xi.at[:, 0].add(lam_r * h0i + lam_i * h0r)
    ar_t = jnp.broadcast_to(lam_r, xr.shape)
    ai_t = jnp.broadcast_to(lam_i, xr.shape)
    _, _, sr, si = lax.associative_scan(_complex_affine_combine, (ar_t, ai_t, xr, xi), axis=1)
    y = (jnp.einsum('ghp,btgp->btgh', c_re.astype(F32), sr)
         - jnp.einsum('ghp,btgp->btgh', c_im.astype(F32), si))
    y = y.reshape(b, t, SSM_WIDTH) + d_skip.astype(F32) * u
    y = jax.nn.gelu(y)
    y = y * jax.nn.sigmoid(y @ w_glu.astype(F32))
    out = (y @ w_out.astype(F32)).astype(h.dtype)
    return out, (sr[:, -1].astype(h.dtype), si[:, -1].astype(h.dtype))


def conv_ffn(h, prev, w_up, w_dw, b_dw, w_down):
    b, t, _ = h.shape
    up = h @ w_up
    if prev is None:
        prev = jnp.zeros((b, CONV_W - 1, up.shape[-1]), up.dtype)
    full = jnp.concatenate([prev.astype(up.dtype), up], axis=1)
    mixed = b_dw
    for k in range(CONV_W):
        mixed = mixed + w_dw[k] * full[:, k:k + t]
    gate = mixed[..., :FFN_HIDDEN]
    val = mixed[..., FFN_HIDDEN:]
    out = (jax.nn.silu(gate) * val) @ w_down
    return out.astype(h.dtype), full[:, t:]


def gather_pages(pool, page_table):
    g = pool[page_table]
    return g.reshape(g.shape[0], g.shape[1] * g.shape[2], *g.shape[3:])


def setup_inputs(seed: int = 0) -> dict:
    key = jax.random.key(seed)
    ks = iter(jax.random.split(key, 48))

    def nrm(shape, scale=1.0):
        return jax.random.normal(next(ks), shape, F32) * scale

    n_pages = PAST_LEN // PAGE_SIZE
    n_used = DEC_BATCH * n_pages
    n_pool = n_used + n_used // 4
    page_table = jax.random.permutation(next(ks), n_pool)[:n_used].reshape(DEC_BATCH, n_pages).astype(jnp.int32)
    ffn2 = 2 * FFN_HIDDEN
    return {
        'x_prompt': nrm((BATCH, SEQ, D_MODEL)),
        'x_sample': nrm((DEC_BATCH, DEC_SEQ, D_MODEL)),
        'cache_sb_k': nrm((N_ATT_LAYERS, n_pool, PAGE_SIZE, SB_KV_HEADS, SB_HEAD_DIM)),
        'cache_sb_v': nrm((N_ATT_LAYERS, n_pool, PAGE_SIZE, SB_KV_HEADS, SB_HEAD_DIM)),
        'cache_mla_ckv': nrm((N_ATT_LAYERS, n_pool, PAGE_SIZE, MLA_KV_LORA)),
        'cache_mla_kpe': nrm((N_ATT_LAYERS, n_pool, PAGE_SIZE, MLA_ROPE)),
        'page_table': page_table,
        'state_ssm_re': nrm((N_SSM_LAYERS, DEC_BATCH, SSM_GROUPS, SSM_STATE), 0.1),
        'state_ssm_im': nrm((N_SSM_LAYERS, DEC_BATCH, SSM_GROUPS, SSM_STATE), 0.1),
        'state_ffn_conv': nrm((DEPTH, DEC_BATCH, CONV_W - 1, ffn2)),
        'norm_mix': 1.0 + nrm((DEPTH, D_MODEL), 0.01),
        'norm_ffn': 1.0 + nrm((DEPTH, D_MODEL), 0.01),
        'norm_final': 1.0 + nrm((D_MODEL,), 0.01),
        'attn_w_in': nrm((N_ATT_LAYERS, D_MODEL, ATT_IN), D_MODEL ** -0.5),
        'attn_g_q': 1.0 + nrm((N_ATT_LAYERS, MLA_Q_LORA), 0.01),
        'attn_g_kv': 1.0 + nrm((N_ATT_LAYERS, MLA_KV_LORA), 0.01),
        'attn_w_uq': nrm((N_ATT_LAYERS, MLA_Q_LORA, MLA_HEADS * (MLA_NOPE + MLA_ROPE)), MLA_Q_LORA ** -0.5),
        'attn_w_ukv': nrm((N_ATT_LAYERS, MLA_KV_LORA, MLA_HEADS, MLA_NOPE + MLA_V), MLA_KV_LORA ** -0.5),
        'attn_w_out': nrm((N_ATT_LAYERS, ATT_MIX, D_MODEL), ATT_MIX ** -0.5),
        'ssm_w_in': nrm((N_SSM_LAYERS, D_MODEL, SSM_WIDTH), D_MODEL ** -0.5),
        'ssm_a_re': -0.5 + nrm((N_SSM_LAYERS, SSM_GROUPS, SSM_STATE), 0.01),
        'ssm_a_im': math.pi * jnp.arange(SSM_STATE, dtype=F32) + nrm((N_SSM_LAYERS, SSM_GROUPS, SSM_STATE), 0.01),
        'ssm_log_step': jax.random.uniform(next(ks), (N_SSM_LAYERS, SSM_GROUPS), F32, math.log(1e-3), math.log(1e-1)),
        'ssm_b_re': nrm((N_SSM_LAYERS, SSM_GROUPS, SSM_STATE, SSM_GROUP), (2 * SSM_GROUP) ** -0.5),
        'ssm_b_im': nrm((N_SSM_LAYERS, SSM_GROUPS, SSM_STATE, SSM_GROUP), (2 * SSM_GROUP) ** -0.5),
        'ssm_c_re': nrm((N_SSM_LAYERS, SSM_GROUPS, SSM_GROUP, SSM_STATE), SSM_STATE ** -0.5),
        'ssm_c_im': nrm((N_SSM_LAYERS, SSM_GROUPS, SSM_GROUP, SSM_STATE), SSM_STATE ** -0.5),
        'ssm_d': nrm((N_SSM_LAYERS, SSM_WIDTH)),
        'ssm_w_glu': nrm((N_SSM_LAYERS, SSM_WIDTH, SSM_WIDTH), SSM_WIDTH ** -0.5),
        'ssm_w_out': nrm((N_SSM_LAYERS, SSM_WIDTH, D_MODEL), SSM_WIDTH ** -0.5),
        'ffn_w_up': nrm((DEPTH, D_MODEL, ffn2), D_MODEL ** -0.5),
        'ffn_w_dw': nrm((DEPTH, CONV_W, ffn2), CONV_W ** -0.5),
        'ffn_b_dw': nrm((DEPTH, ffn2), 0.01),
        'ffn_w_down': nrm((DEPTH, FFN_HIDDEN, D_MODEL), FFN_HIDDEN ** -0.5),
    }


def reference(x_prompt, x_sample, cache_sb_k, cache_sb_v, cache_mla_ckv, cache_mla_kpe, page_table,
              state_ssm_re, state_ssm_im, state_ffn_conv, norm_mix, norm_ffn, norm_final,
              attn_w_in, attn_g_q, attn_g_kv, attn_w_uq, attn_w_ukv, attn_w_out,
              ssm_w_in, ssm_a_re, ssm_a_im, ssm_log_step, ssm_b_re, ssm_b_im, ssm_c_re, ssm_c_im,
              ssm_d, ssm_w_glu, ssm_w_out, ffn_w_up, ffn_w_dw, ffn_b_dw, ffn_w_down):
    t_p = x_prompt.shape[1]
    t_s = x_sample.shape[1]
    past_len = page_table.shape[1] * cache_sb_k.shape[2]
    pos_p = jnp.arange(t_p, dtype=jnp.int32)
    pos_s = past_len + jnp.arange(t_s, dtype=jnp.int32)
    kpos_s = jnp.arange(past_len + t_s, dtype=jnp.int32)
    xp, xs = x_prompt, x_sample
    rows_p, rows_s, ssm_p, ssm_s, conv_p, conv_s = [], [], [], [], [], []
    for layer in range(DEPTH):
        i = layer // 2
        hp = rmsnorm(xp, norm_mix[layer])
        hs = rmsnorm(xs, norm_mix[layer])
        if layer % 2 == 0:
            w = (attn_w_in[i], attn_g_q[i], attn_g_kv[i], attn_w_uq[i], attn_w_ukv[i], attn_w_out[i])
            past = (gather_pages(cache_sb_k[i], page_table), gather_pages(cache_sb_v[i], page_table),
                    gather_pages(cache_mla_ckv[i], page_table), gather_pages(cache_mla_kpe[i], page_table))
            mp, rp = attn_mixer(hp, pos_p, pos_p, None, *w)
            ms, rs = attn_mixer(hs, pos_s, kpos_s, past, *w)
            rows_p.append(rp)
            rows_s.append(rs)
        else:
            w = (ssm_w_in[i], ssm_a_re[i], ssm_a_im[i], ssm_log_step[i], ssm_b_re[i], ssm_b_im[i],
                 ssm_c_re[i], ssm_c_im[i], ssm_d[i], ssm_w_glu[i], ssm_w_out[i])
            mp, sp = ssm_mixer(hp, None, *w)
            ms, ss = ssm_mixer(hs, (state_ssm_re[i], state_ssm_im[i]), *w)
            ssm_p.append(sp)
            ssm_s.append(ss)
        xp = xp + mp
        xs = xs + ms
        fw = (ffn_w_up[layer], ffn_w_dw[layer], ffn_b_dw[layer], ffn_w_down[layer])
        fp, cp = conv_ffn(rmsnorm(xp, norm_ffn[layer]), None, *fw)
        fs, cs = conv_ffn(rmsnorm(xs, norm_ffn[layer]), state_ffn_conv[layer], *fw)
        xp = xp + fp
        xs = xs + fs
        conv_p.append(cp)
        conv_s.append(cs)
    y_prompt = rmsnorm(xp, norm_final)
    y_sample = rmsnorm(xs, norm_final)
    p_sb_k = jnp.stack([r[0] for r in rows_p])
    p_sb_v = jnp.stack([r[1] for r in rows_p])
    p_ckv = jnp.stack([r[2] for r in rows_p])
    p_kpe = jnp.stack([r[3] for r in rows_p])
    p_ssm_re = jnp.stack([s[0] for s in ssm_p])
    p_ssm_im = jnp.stack([s[1] for s in ssm_p])
    p_conv = jnp.stack(conv_p)
    s_sb_k = jnp.stack([r[0] for r in rows_s])
    s_sb_v = jnp.stack([r[1] for r in rows_s])
    s_ckv = jnp.stack([r[2] for r in rows_s])
    s_kpe = jnp.stack([r[3] for r in rows_s])
    s_ssm_re = jnp.stack([s[0] for s in ssm_s])
    s_ssm_im = jnp.stack([s[1] for s in ssm_s])
    s_conv = jnp.stack(conv_s)
    return (y_prompt, y_sample, p_sb_k, p_sb_v, p_ckv, p_kpe, p_ssm_re, p_ssm_im, p_conv,
            s_sb_k, s_sb_v, s_ckv, s_kpe, s_ssm_re, s_ssm_im, s_conv)
```

```python
import functools
import math

import jax
import jax.numpy as jnp
from jax import lax
from jax.experimental import pallas as pl
from jax.experimental.pallas import tpu as pltpu

F32 = jnp.float32
BF16 = jnp.bfloat16
EPS = 1e-6

SB_KV_HEADS = 2
SB_REP = 4
SB_HEAD_DIM = 128
SB_SCALE = SB_HEAD_DIM ** -0.5
MLA_HEADS = 8
MLA_Q_LORA = 768
MLA_KV_LORA = 512
MLA_NOPE = 128
MLA_ROPE = 64
MLA_V = 128
MLA_SCALE = (MLA_NOPE + MLA_ROPE) ** -0.5
ROPE_BASE = 10000.0
SSM_GROUP = 16
SSM_STATE = 64
CONV_W = 3

V7X_VMEM_BYTES = 64 * 2 ** 20
VMEM_LIMIT_BYTES = V7X_VMEM_BYTES - 8 * 2 ** 20
LANES = 128
SUBLANES = 8

SB_QW = SB_KV_HEADS * SB_REP * SB_HEAD_DIM
SB_KW = SB_KV_HEADS * SB_HEAD_DIM
KC_W = MLA_KV_LORA + LANES
NEG_BIG = -1e30


def _cparams(*sem):
    return pltpu.CompilerParams(dimension_semantics=sem, vmem_limit_bytes=VMEM_LIMIT_BYTES)


def _dot(a, b):
    return jnp.dot(a, b, preferred_element_type=F32)


def _dot_nt(a, b):
    return lax.dot_general(a, b, (((1,), (1,)), ((), ())), preferred_element_type=F32)


def _pick(n, pref):
    t = min(n, pref)
    while n % t:
        t //= 2
    return t


def _rmsnorm_body(x_ref, g_ref, o_ref):
    x = x_ref[...]
    y = x * lax.rsqrt(jnp.mean(x * x, axis=-1, keepdims=True) + EPS)
    o_ref[...] = (y * g_ref[...]).astype(o_ref.dtype)


def rmsnorm_rows(x, g, out_dtype, row_start=0, n_rows=None, tm=512):
    m, d = x.shape
    n_rows = m - row_start if n_rows is None else n_rows
    tm = _pick(math.gcd(n_rows, row_start) if row_start else n_rows, tm)
    off = row_start // tm
    return pl.pallas_call(
        _rmsnorm_body,
        grid=(n_rows // tm,),
        in_specs=[pl.BlockSpec((tm, d), lambda i: (i + off, 0)),
                  pl.BlockSpec((1, d), lambda i: (0, 0))],
        out_specs=pl.BlockSpec((tm, d), lambda i: (i, 0)),
        out_shape=jax.ShapeDtypeStruct((n_rows, d), out_dtype),
        compiler_params=_cparams("parallel"),
        name="rmsnorm",
    )(x, g.reshape(1, d).astype(F32))


def _mm_body(a_ref, w_ref, o_ref):
    o_ref[...] = _dot(a_ref[...], w_ref[...]).astype(o_ref.dtype)


def _mm_resid_body(a_ref, w_ref, r_ref, o_ref):
    o_ref[...] = r_ref[...] + _dot(a_ref[...], w_ref[...])


def _glu_body(a_ref, y_ref, w_ref, o_ref):
    y = y_ref[...]
    gate = 1.0 / (1.0 + jnp.exp(-_dot(a_ref[...], w_ref[...])))
    o_ref[...] = (y * gate).astype(o_ref.dtype)


def matmul(a, w, out_dtype=F32, resid=None, glu_y=None, tm=512, tn=512):
    m, k = a.shape
    n = w.shape[1]
    tm, tn = _pick(m, tm), _pick(n, tn)
    a_spec = pl.BlockSpec((tm, k), lambda i, j: (i, 0))
    w_spec = pl.BlockSpec((k, tn), lambda i, j: (0, j))
    t_spec = pl.BlockSpec((tm, tn), lambda i, j: (i, j))
    if resid is not None:
        body, specs, args = _mm_resid_body, [a_spec, w_spec, t_spec], (a, w, resid)
    elif glu_y is not None:
        body, specs, args = _glu_body, [a_spec, t_spec, w_spec], (a, glu_y, w)
    else:
        body, specs, args = _mm_body, [a_spec, w_spec], (a, w)
    return pl.pallas_call(
        body,
        grid=(m // tm, n // tn),
        in_specs=specs,
        out_specs=t_spec,
        out_shape=jax.ShapeDtypeStruct((m, n), out_dtype),
        compiler_params=_cparams("parallel", "arbitrary"),
        name="matmul",
    )(*args)


_O_K = SB_QW
_O_V = _O_K + SB_KW
_O_CQ = _O_V + SB_KW
_O_CKV = _O_CQ + MLA_Q_LORA
_O_PE = _O_CKV + MLA_KV_LORA
_O_PESW = _O_PE + LANES
ATT_IN_EXT = _O_PESW + LANES


def _attn_in_body(h_ref, w_ref, gq_ref, gkv_ref, cos_ref, sin_ref,
                  qsb_ref, k_ref, v_ref, cq_ref, ckv_ref, kpe_ref, kc_ref):
    h = h_ref[...]

    def seg(lo, hi):
        return _dot(h, w_ref[:, lo:hi])

    def norm(x, g_ref):
        return x * lax.rsqrt(jnp.mean(x * x, axis=-1, keepdims=True) + EPS) * g_ref[...]

    qsb_ref[...] = seg(0, _O_K).astype(qsb_ref.dtype)
    k_ref[...] = seg(_O_K, _O_V)
    v_ref[...] = seg(_O_V, _O_CQ)
    cq_ref[...] = norm(seg(_O_CQ, _O_CKV), gq_ref).astype(cq_ref.dtype)
    ckv = norm(seg(_O_CKV, _O_PE), gkv_ref)
    ckv_ref[...] = ckv
    kpe = seg(_O_PE, _O_PESW) * cos_ref[...] + seg(_O_PESW, ATT_IN_EXT) * sin_ref[...]
    kpe_ref[...] = kpe[:, :MLA_ROPE]
    kc_ref[:, :MLA_KV_LORA] = ckv.astype(kc_ref.dtype)
    kc_ref[:, MLA_KV_LORA:] = kpe.astype(kc_ref.dtype)


def attn_in(h, w_ext, g_q, g_kv, cos_t, sin_t, tm=256):
    m, d = h.shape
    tm = _pick(m, tm)
    row = lambda w: pl.BlockSpec((tm, w), lambda i: (i, 0))
    full = lambda a: pl.BlockSpec(a.shape, lambda i: (0,) * a.ndim)
    outs = [(SB_QW, BF16), (SB_KW, F32), (SB_KW, F32), (MLA_Q_LORA, BF16),
            (MLA_KV_LORA, F32), (MLA_ROPE, F32), (KC_W, BF16)]
    return pl.pallas_call(
        _attn_in_body,
        grid=(m // tm,),
        in_specs=[row(d), full(w_ext), full(g_q), full(g_kv), row(LANES), row(LANES)],
        out_specs=[row(w) for w, _ in outs],
        out_shape=[jax.ShapeDtypeStruct((m, w), dt) for w, dt in outs],
        compiler_params=_cparams("parallel"),
        name="attn_in",
    )(h, w_ext, g_q, g_kv, cos_t, sin_t)


def _mla_q_body(cq_ref, wuq_ref, wuk_ref, cos_ref, sin_ref, o_ref):
    cq = cq_ref[...]
    hw = MLA_HEADS * LANES
    q_nope = _dot(cq, wuq_ref[:, :hw]).astype(BF16)
    pe = _dot(cq, wuq_ref[:, hw:2 * hw])
    pe_sw = _dot(cq, wuq_ref[:, 2 * hw:])
    cos = cos_ref[...]
    sin = sin_ref[...]
    for h in range(MLA_HEADS):
        sl = slice(h * LANES, (h + 1) * LANES)
        o_ref[h, :, :MLA_KV_LORA] = _dot(q_nope[:, sl], wuk_ref[h]).astype(o_ref.dtype)
        o_ref[h, :, MLA_KV_LORA:] = (pe[:, sl] * cos + pe_sw[:, sl] * sin).astype(o_ref.dtype)


def mla_q(cqn, wuq_ext, wuk_t, cos_t, sin_t, tm=256):
    m = cqn.shape[0]
    tm = _pick(m, tm)
    row = lambda w: pl.BlockSpec((tm, w), lambda i: (i, 0))
    full = lambda a: pl.BlockSpec(a.shape, lambda i: (0,) * a.ndim)
    return pl.pallas_call(
        _mla_q_body,
        grid=(m // tm,),
        in_specs=[row(MLA_Q_LORA), full(wuq_ext), full(wuk_t), row(LANES), row(LANES)],
        out_specs=pl.BlockSpec((MLA_HEADS, tm, KC_W), lambda i: (0, i, 0)),
        out_shape=jax.ShapeDtypeStruct((MLA_HEADS, m, KC_W), BF16),
        compiler_params=_cparams("parallel"),
        name="mla_q",
    )(cqn, wuq_ext, wuk_t, cos_t, sin_t)


def _split3(x):
    hi = x.astype(BF16)
    r = x - hi.astype(F32)
    mid = r.astype(BF16)
    lo = (r - mid.astype(F32)).astype(BF16)
    return hi, mid, lo


def _later_sum(ls, tri):
    hi, mid, lo = _split3(ls)
    return _dot(hi, tri) + _dot(mid, tri) + _dot(lo, tri)


def _stick_weights(z, vis, carry, tri):
    soft = jnp.log1p(jnp.exp(-jnp.abs(z)))
    log_beta = jnp.minimum(z, 0.0) - soft
    log_stay = log_beta - z
    if vis is not None:
        log_stay = jnp.where(vis, log_stay, 0.0)
    w = jnp.exp(log_beta + _later_sum(log_stay, tri) + carry)
    if vis is not None:
        w = jnp.where(vis, w, 0.0)
    return w, carry + jnp.sum(log_stay, axis=-1, keepdims=True)


def _sb_prompt_body(q_ref, k_ref, v_ref, tri_ref, o_ref, *, tq):
    qi = pl.program_id(2)
    q = q_ref[...]
    qs = jnp.concatenate([q[:, r * SB_HEAD_DIM:(r + 1) * SB_HEAD_DIM] for r in range(SB_REP)], axis=0)
    rows = SB_REP * tq
    q_pos = qi * tq + lax.broadcasted_iota(jnp.int32, (rows, tq), 0) % tq
    k_off = lax.broadcasted_iota(jnp.int32, (rows, tq), 1)
    tri = tri_ref[...]

    def step(i, state):
        acc, carry = state
        kb = qi - i
        start = pl.multiple_of(kb * tq, tq)
        k = k_ref[pl.ds(start, tq), :].astype(BF16)
        v = v_ref[pl.ds(start, tq), :].astype(BF16)
        z = _dot_nt(qs, k) * SB_SCALE
        vis = (kb * tq + k_off) < q_pos
        w, carry = _stick_weights(z, vis, carry, tri)
        return acc + _dot(w.astype(BF16), v), carry

    acc, _ = lax.fori_loop(0, qi + 1, step,
                           (jnp.zeros((rows, SB_HEAD_DIM), F32), jnp.zeros((rows, 1), F32)))
    for r in range(SB_REP):
        o_ref[:, r * SB_HEAD_DIM:(r + 1) * SB_HEAD_DIM] = acc[r * tq:(r + 1) * tq].astype(o_ref.dtype)


def sb_prompt_attention(q_sb, k_sb, v_sb, tri, n_seq, seq_len, tq):
    nq = seq_len // tq
    gw = SB_REP * SB_HEAD_DIM
    return pl.pallas_call(
        functools.partial(_sb_prompt_body, tq=tq),
        grid=(n_seq, SB_KV_HEADS, nq),
        in_specs=[pl.BlockSpec((tq, gw), lambda b, g, i: (b * nq + i, g)),
                  pl.BlockSpec((seq_len, SB_HEAD_DIM), lambda b, g, i: (b, g)),
                  pl.BlockSpec((seq_len, SB_HEAD_DIM), lambda b, g, i: (b, g)),
                  pl.BlockSpec(tri.shape, lambda b, g, i: (0, 0))],
        out_specs=pl.BlockSpec((tq, gw), lambda b, g, i: (b * nq + i, g)),
        out_shape=jax.ShapeDtypeStruct((n_seq * seq_len, SB_QW), BF16),
        compiler_params=_cparams("parallel", "parallel", "arbitrary"),
        name="sb_prompt_attention",
    )(q_sb, k_sb, v_sb, tri)


def _mla_prompt_body(q_ref, kc_ref, wuv_ref, o_ref, m_ref, l_ref, acc_ref, *, tq, tk):
    qi = pl.program_id(1)
    rows = MLA_HEADS * tq
    q = q_ref[...].reshape(rows, KC_W)
    q_pos = qi * tq + lax.broadcasted_iota(jnp.int32, (rows, tk), 0) % tq
    k_off = lax.broadcasted_iota(jnp.int32, (rows, tk), 1)
    m_ref[...] = jnp.full(m_ref.shape, NEG_BIG, F32)
    l_ref[...] = jnp.zeros(l_ref.shape, F32)
    acc_ref[...] = jnp.zeros(acc_ref.shape, F32)

    def step(kb, _):
        start = pl.multiple_of(kb * tk, tk)
        kc = kc_ref[pl.ds(start, tk), :]
        s = _dot_nt(q, kc) * MLA_SCALE
        s = jnp.where((kb * tk + k_off) <= q_pos, s, -jnp.inf)
        m_old = m_ref[...]
        m_new = jnp.maximum(m_old, jnp.max(s, axis=-1, keepdims=True))
        alpha = jnp.exp(m_old - m_new)
        p = jnp.exp(s - m_new)
        l_ref[...] = alpha * l_ref[...] + jnp.sum(p, axis=-1, keepdims=True)
        acc_ref[...] = alpha * acc_ref[...] + _dot(p.astype(BF16), kc[:, :MLA_KV_LORA])
        m_ref[...] = m_new
        return 0

    n_kb = ((qi + 1) * tq + tk - 1) // tk
    lax.fori_loop(0, n_kb, step, 0)
    o_lat = (acc_ref[...] / l_ref[...]).astype(BF16)
    for h in range(MLA_HEADS):
        o_ref[:, h * MLA_V:(h + 1) * MLA_V] = _dot(o_lat[h * tq:(h + 1) * tq], wuv_ref[h]).astype(o_ref.dtype)


def mla_prompt_attention(qcat, kc, wuv, n_seq, seq_len, tq, tk):
    nq = seq_len // tq
    rows = MLA_HEADS * tq
    return pl.pallas_call(
        functools.partial(_mla_prompt_body, tq=tq, tk=tk),
        grid=(n_seq, nq),
        in_specs=[pl.BlockSpec((MLA_HEADS, tq, KC_W), lambda b, i: (0, b * nq + i, 0)),
                  pl.BlockSpec((seq_len, KC_W), lambda b, i: (b, 0)),
                  pl.BlockSpec(wuv.shape, lambda b, i: (0, 0, 0))],
        out_specs=pl.BlockSpec((tq, MLA_HEADS * MLA_V), lambda b, i: (b * nq + i, 0)),
        out_shape=jax.ShapeDtypeStruct((n_seq * seq_len, MLA_HEADS * MLA_V), BF16),
        scratch_shapes=[pltpu.VMEM((rows, 1), F32), pltpu.VMEM((rows, 1), F32),
                        pltpu.VMEM((rows, MLA_KV_LORA), F32)],
        compiler_params=_cparams("parallel", "arbitrary"),
        name="mla_prompt_attention",
    )(qcat, kc, wuv)


def _decode_body(pt_ref, qsb_ref, qa_ref, qp_ref, nk_ref, nv_ref, nc_ref, nr_ref, wuv_ref, tri_ref,
                 ck_hbm, cv_hbm, cc_hbm, cr_hbm,
                 osb_ref, omla_ref,
                 kbuf, vbuf, cbuf, rbuf, sems, padk, padv, padc, padr,
                 sbacc_ref, carry_ref, m_ref, l_ref, acc_ref,
                 *, layer, n_seq, n_pages, ppc, page, t_new):
    b = pl.program_id(0)
    c = pl.program_id(1)
    n_chunks = n_pages // ppc
    n = b * n_chunks + c
    slot = n % 2
    hbm = (ck_hbm, cv_hbm, cc_hbm, cr_hbm)
    bufs = (kbuf, vbuf, cbuf, rbuf)
    qrows = SB_KV_HEADS * SB_REP * t_new
    hr = SB_REP * t_new
    kvrows = SB_KV_HEADS * page

    def page_copies(bb, cc, sl):
        first = bb * n_pages + (n_chunks - 1 - cc) * ppc
        out = []
        for j in range(ppc):
            pid = pt_ref[first + j]
            for a in range(4):
                out.append(pltpu.make_async_copy(hbm[a].at[layer, pid], bufs[a].at[sl, j], sems.at[sl, a]))
        return out

    @pl.when(n == 0)
    def _():
        for cp in page_copies(0, 0, 0):
            cp.start()
        for ref in (padk, padv, padc, padr):
            ref[...] = jnp.zeros(ref.shape, F32)

    @pl.when(n + 1 < n_seq * n_chunks)
    def _():
        nxt = n + 1
        for cp in page_copies(nxt // n_chunks, nxt % n_chunks, 1 - slot):
            cp.start()

    q_sb = qsb_ref[0].astype(F32)
    q_a = qa_ref[0].astype(F32)
    q_p = qp_ref[0].astype(F32)
    tri = tri_ref[...]
    own_head = (lax.broadcasted_iota(jnp.int32, (qrows, kvrows), 1) % SB_KV_HEADS
                == lax.broadcasted_iota(jnp.int32, (qrows, kvrows), 0) // hr)

    def attend(k, v, lat, rope_t, sb_vis, mla_vis):
        z = _dot_nt(q_sb, k) * SB_SCALE
        w, carry = _stick_weights(z, sb_vis, carry_ref[...], tri)
        carry_ref[...] = carry
        sbacc_ref[...] += _dot(w, v)
        s = (_dot_nt(q_a, lat) + _dot(q_p, rope_t)) * MLA_SCALE
        if mla_vis is not None:
            s = jnp.where(mla_vis, s, -jnp.inf)
        m_old = m_ref[...]
        m_new = jnp.maximum(m_old, jnp.max(s, axis=-1, keepdims=True))
        alpha = jnp.exp(m_old - m_new)
        p = jnp.exp(s - m_new)
        l_ref[...] = alpha * l_ref[...] + jnp.sum(p, axis=-1, keepdims=True)
        acc_ref[...] = alpha * acc_ref[...] + _dot(p, lat)
        m_ref[...] = m_new

    @pl.when(c == 0)
    def _():
        sbacc_ref[...] = jnp.zeros(sbacc_ref.shape, F32)
        carry_ref[...] = jnp.zeros(carry_ref.shape, F32)
        m_ref[...] = jnp.full(m_ref.shape, NEG_BIG, F32)
        l_ref[...] = jnp.zeros(l_ref.shape, F32)
        acc_ref[...] = jnp.zeros(acc_ref.shape, F32)
        padk[:SB_KV_HEADS * t_new] = nk_ref[0]
        padv[:SB_KV_HEADS * t_new] = nv_ref[0]
        padc[:t_new] = nc_ref[0]
        padr[:, :t_new] = nr_ref[0]
        tok_sb = lax.broadcasted_iota(jnp.int32, (qrows, kvrows), 0) % t_new
        key_sb = lax.broadcasted_iota(jnp.int32, (qrows, kvrows), 1) // SB_KV_HEADS
        tok = lax.broadcasted_iota(jnp.int32, (qrows, page), 0) % t_new
        key = lax.broadcasted_iota(jnp.int32, (qrows, page), 1)
        attend(padk[...], padv[...], padc[...], padr[...], own_head & (key_sb < tok_sb), key <= tok)

    for cp in page_copies(b, c, slot):
        cp.wait()

    for j in range(ppc - 1, -1, -1):
        attend(kbuf[slot, j], vbuf[slot, j], cbuf[slot, j], rbuf[slot, j], own_head, None)

    @pl.when(c == n_chunks - 1)
    def _():
        osb_ref[0] = sbacc_ref[...].astype(osb_ref.dtype)
        o_lat = (acc_ref[...] / l_ref[...]).astype(BF16)
        omla_ref[0] = jnp.concatenate(
            [_dot(o_lat[h * t_new:(h + 1) * t_new], wuv_ref[h]) for h in range(MLA_HEADS)],
            axis=0).astype(omla_ref.dtype)


def decode_attention(page_table, q_sb, q_a, q_p, new_k, new_v, new_c, new_r, wuv, tri,
                     cache_k, cache_v, cache_c, cache_r, layer, ppc):
    n_seq, n_pages = page_table.shape
    t_new = new_c.shape[1]
    page = cache_c.shape[2]
    kvrows = SB_KV_HEADS * page
    assert tri.shape[0] == kvrows and n_pages % ppc == 0 and t_new <= page
    assert SB_KV_HEADS * SB_REP == MLA_HEADS
    n_chunks = n_pages // ppc
    qrows = MLA_HEADS * t_new
    blk = lambda a: pl.BlockSpec((1,) + a.shape[1:], lambda b, c, pt: (b,) + (0,) * (a.ndim - 1))
    full = lambda a: pl.BlockSpec(a.shape, lambda b, c, pt: (0,) * a.ndim)
    any_spec = pl.BlockSpec(memory_space=pl.ANY)
    grid_spec = pltpu.PrefetchScalarGridSpec(
        num_scalar_prefetch=1,
        grid=(n_seq, n_chunks),
        in_specs=[blk(q_sb), blk(q_a), blk(q_p), blk(new_k), blk(new_v), blk(new_c), blk(new_r),
                  full(wuv), full(tri), any_spec, any_spec, any_spec, any_spec],
        out_specs=[pl.BlockSpec((1, qrows, SB_HEAD_DIM), lambda b, c, pt: (b, 0, 0)),
                   pl.BlockSpec((1, qrows, MLA_V), lambda b, c, pt: (b, 0, 0))],
        scratch_shapes=[pltpu.VMEM((2, ppc, kvrows, SB_HEAD_DIM), F32), pltpu.VMEM((2, ppc, kvrows, SB_HEAD_DIM), F32),
                        pltpu.VMEM((2, ppc, page, MLA_KV_LORA), F32), pltpu.VMEM((2, ppc, MLA_ROPE, page), F32),
                        pltpu.SemaphoreType.DMA((2, 4)),
                        pltpu.VMEM((kvrows, SB_HEAD_DIM), F32), pltpu.VMEM((kvrows, SB_HEAD_DIM), F32),
                        pltpu.VMEM((page, MLA_KV_LORA), F32), pltpu.VMEM((MLA_ROPE, page), F32),
                        pltpu.VMEM((qrows, SB_HEAD_DIM), F32), pltpu.VMEM((qrows, 1), F32),
                        pltpu.VMEM((qrows, 1), F32), pltpu.VMEM((qrows, 1), F32),
                        pltpu.VMEM((qrows, MLA_KV_LORA), F32)],
    )
    return pl.pallas_call(
        functools.partial(_decode_body, layer=layer, n_seq=n_seq, n_pages=n_pages, ppc=ppc,
                          page=page, t_new=t_new),
        grid_spec=grid_spec,
        out_shape=[jax.ShapeDtypeStruct((n_seq, qrows, SB_HEAD_DIM), BF16),
                   jax.ShapeDtypeStruct((n_seq, qrows, MLA_V), BF16)],
        compiler_params=_cparams("arbitrary", "arbitrary"),
        name="decode_attention",
    )(page_table.reshape(-1), q_sb, q_a, q_p, new_k, new_v, new_c, new_r, wuv, tri,
      cache_k, cache_v, cache_c, cache_r)


def _zoh_body(are_ref, aim_ref, step_ref, bre_ref, bim_ref, lr_ref, li_ref, br_ref, bi_ref):
    dt = jnp.exp(step_ref[...])
    ar = are_ref[...]
    ai = aim_ref[...]
    mag = jnp.exp(ar * dt)
    lam_r = mag * jnp.cos(ai * dt)
    lam_i = mag * jnp.sin(ai * dt)
    den = ar * ar + ai * ai
    nr = lam_r - 1.0
    fr = (nr * ar + lam_i * ai) / den
    fi = (lam_i * ar - nr * ai) / den
    lr_ref[...] = lam_r
    li_ref[...] = lam_i
    for h in range(SSM_GROUP):
        br = bre_ref[h]
        bi = bim_ref[h]
        br_ref[h] = fr * br - fi * bi
        bi_ref[h] = fr * bi + fi * br


def zoh(a_re, a_im, log_step, b_re, b_im):
    g, p = a_re.shape
    hgp = jax.ShapeDtypeStruct((SSM_GROUP, g, p), F32)
    gp = jax.ShapeDtypeStruct((g, p), F32)
    return pl.pallas_call(_zoh_body, out_shape=[gp, gp, hgp, hgp], name="ssm_zoh")(
        a_re, a_im, log_step.reshape(g, 1), jnp.transpose(b_re, (2, 0, 1)), jnp.transpose(b_im, (2, 0, 1)))


def _gelu_tanh(y):
    return 0.5 * y * (1.0 + jnp.tanh(math.sqrt(2.0 / math.pi) * (y + 0.044715 * (y * y * y))))


def _ssm_body(u_ref, bre_ref, bim_ref, cre_ref, cim_ref, d_ref, lr_ref, li_ref, h0r_ref, h0i_ref,
              y32_ref, y16_ref, sr_ref, si_ref, xr_s, xi_s, *, nb, steps):
    tc = pl.program_id(2)
    u = u_ref[0]
    u16 = u.astype(BF16)
    xr_s[...] = _dot(u16, bre_ref[0])
    xi_s[...] = _dot(u16, bim_ref[0])
    lam_r = lr_ref[0]
    lam_i = li_ref[0]

    @pl.when(tc == 0)
    def _():
        sr_ref[0] = h0r_ref[0]
        si_ref[0] = h0i_ref[0]

    def step(t, state):
        sr, si = state
        rows = pl.ds(pl.multiple_of(t * nb, nb), nb)
        nr = lam_r * sr - lam_i * si + xr_s[rows, :]
        ni = lam_r * si + lam_i * sr + xi_s[rows, :]
        xr_s[rows, :] = nr
        xi_s[rows, :] = ni
        return nr, ni

    sr, si = lax.fori_loop(0, steps, step, (sr_ref[0], si_ref[0]))
    sr_ref[0] = sr
    si_ref[0] = si
    y = _dot(xr_s[...].astype(BF16), cre_ref[0]) - _dot(xi_s[...].astype(BF16), cim_ref[0])
    y = _gelu_tanh(y + d_ref[...] * u)
    y32_ref[0] = y
    y16_ref[0] = y.astype(BF16)


def ssm_scan(u, bre, bim, cre, cim, d, lam_r, lam_i, h0r, h0i, nb, steps):
    s_, rows, w = u.shape
    n_blk, cw, sw = bre.shape
    n_t = rows // (steps * nb)
    tr = steps * nb
    return pl.pallas_call(
        functools.partial(_ssm_body, nb=nb, steps=steps),
        grid=(s_, n_blk, n_t),
        in_specs=[pl.BlockSpec((1, tr, cw), lambda s, j, t: (s, t, j)),
                  pl.BlockSpec((1, cw, sw), lambda s, j, t: (j, 0, 0)),
                  pl.BlockSpec((1, cw, sw), lambda s, j, t: (j, 0, 0)),
                  pl.BlockSpec((1, sw, cw), lambda s, j, t: (j, 0, 0)),
                  pl.BlockSpec((1, sw, cw), lambda s, j, t: (j, 0, 0)),
                  pl.BlockSpec((1, cw), lambda s, j, t: (0, j)),
                  pl.BlockSpec((1, 1, sw), lambda s, j, t: (j, 0, 0)),
                  pl.BlockSpec((1, 1, sw), lambda s, j, t: (j, 0, 0)),
                  pl.BlockSpec((1, nb, sw), lambda s, j, t: (s, 0, j)),
                  pl.BlockSpec((1, nb, sw), lambda s, j, t: (s, 0, j))],
        out_specs=[pl.BlockSpec((1, tr, cw), lambda s, j, t: (s, t, j)),
                   pl.BlockSpec((1, tr, cw), lambda s, j, t: (s, t, j)),
                   pl.BlockSpec((1, nb, sw), lambda s, j, t: (s, 0, j)),
                   pl.BlockSpec((1, nb, sw), lambda s, j, t: (s, 0, j))],
        out_shape=[jax.ShapeDtypeStruct(u.shape, F32), jax.ShapeDtypeStruct(u.shape, BF16),
                   jax.ShapeDtypeStruct(h0r.shape, F32), jax.ShapeDtypeStruct(h0i.shape, F32)],
        scratch_shapes=[pltpu.VMEM((tr, sw), F32), pltpu.VMEM((tr, sw), F32)],
        compiler_params=_cparams("parallel", "parallel", "arbitrary"),
        name="ssm_scan",
    )(u, bre, bim, cre, cim, d, lam_r, lam_i, h0r, h0i)


def _ffn_up_body(h_ref, wg_ref, wv_ref, dwg_ref, dwv_ref, bg_ref, bv_ref, eg_ref, ev_ref,
                 hid_ref, tailg_ref, tailv_ref, upg_ref, upv_ref, cg_ref, cv_ref,
                 *, tm, n_prompt_tiles, tiles_per_seq, t_new):
    i = pl.program_id(1)
    h = h_ref[...]
    up_g = _dot(h, wg_ref[...])
    up_v = _dot(h, wv_ref[...])

    def conv(up, m1, m2, dw_ref, b_ref):
        return ((b_ref[...] + dw_ref[0:1] * m2) + dw_ref[1:2] * m1) + dw_ref[2:3] * up

    def act(gate, val):
        return (gate * (1.0 / (1.0 + jnp.exp(-gate))) * val).astype(hid_ref.dtype)

    @pl.when(i < n_prompt_tiles)
    def _():
        @pl.when(i % tiles_per_seq == 0)
        def _():
            cg_ref[...] = jnp.zeros(cg_ref.shape, F32)
            cv_ref[...] = jnp.zeros(cv_ref.shape, F32)

        mixed = []
        fixed = []
        for up, c_ref, dw_ref, b_ref in ((up_g, cg_ref, dwg_ref, bg_ref), (up_v, cv_ref, dwv_ref, bv_ref)):
            mixed.append(conv(up, pltpu.roll(up, 1, 0), pltpu.roll(up, 2, 0), dw_ref, b_ref))
            r8 = lax.broadcasted_iota(jnp.int32, c_ref.shape, 0)
            top = up[:SUBLANES]
            prev = c_ref[...]
            m1 = jnp.where(r8 < 1, pltpu.roll(prev, 1, 0), pltpu.roll(top, 1, 0))
            m2 = jnp.where(r8 < 2, pltpu.roll(prev, 2, 0), pltpu.roll(top, 2, 0))
            fixed.append(conv(top, m1, m2, dw_ref, b_ref))
            c_ref[...] = up[tm - SUBLANES:]
        hid_ref[...] = act(mixed[0], mixed[1])
        hid_ref[:SUBLANES] = act(fixed[0], fixed[1])
        tailg_ref[0] = up_g[tm - SUBLANES:]
        tailv_ref[0] = up_v[tm - SUBLANES:]

    @pl.when(i >= n_prompt_tiles)
    def _():
        rt = lax.broadcasted_iota(jnp.int32, (tm, 1), 0) % t_new
        mixed = []
        for up, e_ref, dw_ref, b_ref in ((up_g, eg_ref, dwg_ref, bg_ref), (up_v, ev_ref, dwv_ref, bv_ref)):
            e = e_ref[...]
            m1 = jnp.where(rt >= 1, pltpu.roll(up, 1, 0), pltpu.roll(e, tm - (t_new - 1), 0))
            m2 = jnp.where(rt >= 2, pltpu.roll(up, 2, 0), pltpu.roll(e, tm - (t_new - 2), 0))
            mixed.append(conv(up, m1, m2, dw_ref, b_ref))
        hid_ref[...] = act(mixed[0], mixed[1])
        upg_ref[...] = up_g
        upv_ref[...] = up_v


def ffn_up(h, wg, wv, dwg, dwv, bg, bv, eg, ev, n_prompt_seq, seq_len, t_new, tm, tn):
    m, d = h.shape
    f = wg.shape[1]
    n_prompt_tiles = n_prompt_seq * seq_len // tm
    tiles_per_seq = seq_len // tm
    ms = m - n_prompt_seq * seq_len
    n_sample_tiles = ms // tm
    assert seq_len % tm == 0 and ms % tm == 0 and tm % t_new == 0 and t_new >= CONV_W - 1
    tn = _pick(f, tn)
    last_seq = n_prompt_seq - 1
    col = lambda rows: pl.BlockSpec((rows, tn), lambda j, i: (0, j))
    samp = pl.BlockSpec((tm, tn), lambda j, i: (jnp.maximum(i - n_prompt_tiles, 0), j))
    tail = pl.BlockSpec((1, SUBLANES, tn), lambda j, i: (jnp.minimum(i // tiles_per_seq, last_seq), 0, j))
    return pl.pallas_call(
        functools.partial(_ffn_up_body, tm=tm, n_prompt_tiles=n_prompt_tiles,
                          tiles_per_seq=tiles_per_seq, t_new=t_new),
        grid=(f // tn, m // tm),
        in_specs=[pl.BlockSpec((tm, d), lambda j, i: (i, 0)),
                  pl.BlockSpec((d, tn), lambda j, i: (0, j)), pl.BlockSpec((d, tn), lambda j, i: (0, j)),
                  col(CONV_W), col(CONV_W), col(1), col(1), samp, samp],
        out_specs=[pl.BlockSpec((tm, tn), lambda j, i: (i, j)), tail, tail, samp, samp],
        out_shape=[jax.ShapeDtypeStruct((m, f), BF16),
                   jax.ShapeDtypeStruct((n_prompt_seq, SUBLANES, f), F32),
                   jax.ShapeDtypeStruct((n_prompt_seq, SUBLANES, f), F32),
                   jax.ShapeDtypeStruct((n_sample_tiles * tm, f), F32),
                   jax.ShapeDtypeStruct((n_sample_tiles * tm, f), F32)],
        scratch_shapes=[pltpu.VMEM((SUBLANES, tn), F32), pltpu.VMEM((SUBLANES, tn), F32)],
        compiler_params=_cparams("parallel", "arbitrary"),
        name="ffn_up",
    )(h, wg, wv, dwg, dwv, bg, bv, eg, ev)


def _rope_tables(pos):
    half = MLA_ROPE // 2
    inv = ROPE_BASE ** (-jnp.arange(half, dtype=F32) / half)
    ang = pos.astype(F32)[:, None] * inv[None, :]
    cos, sin = jnp.cos(ang), jnp.sin(ang)
    zero = jnp.zeros((pos.shape[0], LANES - MLA_ROPE), F32)
    return jnp.concatenate([cos, cos, zero], axis=1), jnp.concatenate([-sin, sin, zero], axis=1)


def _pad_cols(w, width):
    return jnp.pad(w, ((0, 0), (0, width - w.shape[1])))


def _swap_halves(w):
    half = w.shape[1] // 2
    return jnp.concatenate([w[:, half:], w[:, :half]], axis=1)


def _attention_layer(x, hn, i, dims, cos_t, sin_t, tri, page_table, caches, w_in, g_q, g_kv, w_uq, w_ukv, w_out):
    n_p, t_p, n_s, t_s = dims
    mp = n_p * t_p
    pe = w_in[:, _O_PE:]
    w_ext = jnp.concatenate([w_in[:, :_O_PE], _pad_cols(pe, LANES), _pad_cols(_swap_halves(pe), LANES)],
                            axis=1).astype(BF16)
    wq = w_uq.reshape(MLA_Q_LORA, MLA_HEADS, MLA_NOPE + MLA_ROPE)
    wq_pe = wq[:, :, MLA_NOPE:]
    wq_sw = jnp.concatenate([wq_pe[..., MLA_ROPE // 2:], wq_pe[..., :MLA_ROPE // 2]], axis=-1)
    padh = lambda w: jnp.pad(w, ((0, 0), (0, 0), (0, LANES - MLA_ROPE))).reshape(MLA_Q_LORA, -1)
    wuq_ext = jnp.concatenate([wq[:, :, :MLA_NOPE].reshape(MLA_Q_LORA, -1), padh(wq_pe), padh(wq_sw)],
                              axis=1).astype(BF16)
    wuk_t = jnp.transpose(w_ukv[:, :, :MLA_NOPE], (1, 2, 0)).astype(BF16)
    wuv = jnp.transpose(w_ukv[:, :, MLA_NOPE:], (1, 0, 2)).astype(BF16)

    q_sb, k_sb, v_sb, cqn, ckv, kpe, kc = attn_in(
        hn, w_ext, g_q.reshape(1, -1), g_kv.reshape(1, -1), cos_t, sin_t)
    qcat = mla_q(cqn, wuq_ext, wuk_t, cos_t, sin_t)

    tq_sb = _pick(t_p, tri.shape[0])
    o_sb_p = sb_prompt_attention(q_sb, k_sb, v_sb, tri[:tq_sb, :tq_sb], n_p, t_p, tq_sb)
    o_mla_p = mla_prompt_attention(qcat, kc, wuv, n_p, t_p, _pick(t_p, 128), _pick(t_p, 256))

    qs = q_sb[mp:].reshape(n_s, t_s, SB_KV_HEADS, SB_REP, SB_HEAD_DIM)
    qs = jnp.transpose(qs, (0, 2, 3, 1, 4)).reshape(n_s, SB_KV_HEADS * SB_REP * t_s, SB_HEAD_DIM)
    qm = jnp.transpose(qcat[:, mp:].reshape(MLA_HEADS, n_s, t_s, KC_W), (1, 0, 2, 3))
    qm = qm.reshape(n_s, MLA_HEADS * t_s, KC_W)
    new = (k_sb[mp:].reshape(n_s, t_s * SB_KV_HEADS, SB_HEAD_DIM), v_sb[mp:].reshape(n_s, t_s * SB_KV_HEADS, SB_HEAD_DIM),
           ckv[mp:].reshape(n_s, t_s, -1), jnp.transpose(kpe[mp:].reshape(n_s, t_s, -1), (0, 2, 1)))
    n_pool, page = caches[0].shape[1:3]
    o_sb_s, o_mla_s = decode_attention(
        page_table, qs, qm[..., :MLA_KV_LORA], qm[..., MLA_KV_LORA:MLA_KV_LORA + MLA_ROPE], *new,
        wuv, tri,
        caches[0].reshape(-1, n_pool, page * SB_KV_HEADS, SB_HEAD_DIM),
        caches[1].reshape(-1, n_pool, page * SB_KV_HEADS, SB_HEAD_DIM),
        caches[2], jnp.transpose(caches[3], (0, 1, 3, 2)), i, _pick(page_table.shape[1], 16))
    o_sb_s = jnp.transpose(o_sb_s.reshape(n_s, SB_KV_HEADS, SB_REP, t_s, SB_HEAD_DIM), (0, 3, 1, 2, 4))
    o_mla_s = jnp.transpose(o_mla_s.reshape(n_s, MLA_HEADS, t_s, MLA_V), (0, 2, 1, 3))
    o = jnp.concatenate([
        jnp.concatenate([o_sb_p, o_mla_p], axis=1),
        jnp.concatenate([o_sb_s.reshape(n_s * t_s, -1), o_mla_s.reshape(n_s * t_s, -1)], axis=1)], axis=0)
    x = matmul(o, w_out.astype(BF16), resid=x)
    return x, (k_sb, v_sb, ckv, kpe)


def _ssm_layer(x, hn, dims, h0, w_in, a_re, a_im, log_step, b_re, b_im, c_re, c_im, d_skip, w_glu, w_out):
    n_p, t_p, n_s, t_s = dims
    mp = n_p * t_p
    width = w_in.shape[1]
    g, p = a_re.shape
    gpb = 2 * LANES // SSM_GROUP
    n_blk = g // gpb
    u = matmul(hn, w_in.astype(BF16))
    lam_r, lam_i, bbar_r, bbar_i = zoh(a_re, a_im, log_step, b_re, b_im)
    eye = jnp.eye(gpb, dtype=F32)

    def blockdiag_in(bb):
        bb = bb.reshape(SSM_GROUP, n_blk, gpb, p)
        return jnp.einsum('hngp,gk->nghkp', bb, eye).reshape(n_blk, gpb * SSM_GROUP, gpb * p).astype(BF16)

    def blockdiag_out(cc):
        cc = cc.reshape(n_blk, gpb, SSM_GROUP, p)
        return jnp.einsum('nghp,gk->ngpkh', cc, eye).reshape(n_blk, gpb * p, gpb * SSM_GROUP).astype(BF16)

    mats = (blockdiag_in(bbar_r), blockdiag_in(bbar_i), blockdiag_out(c_re), blockdiag_out(c_im),
            d_skip.reshape(1, width), lam_r.reshape(n_blk, 1, gpb * p), lam_i.reshape(n_blk, 1, gpb * p))
    zeros = jnp.zeros((n_p, 1, g * p), F32)
    yp32, yp16, spr, spi = ssm_scan(u[:mp].reshape(n_p, t_p, width), *mats, zeros, zeros,
                                    nb=1, steps=_pick(t_p, 256))
    us = jnp.transpose(u[mp:].reshape(n_s, t_s, width), (1, 0, 2)).reshape(1, t_s * n_s, width)
    ys32, ys16, ssr, ssi = ssm_scan(us, *mats, h0[0].reshape(1, n_s, g * p), h0[1].reshape(1, n_s, g * p),
                                    nb=n_s, steps=t_s)
    back = lambda y: jnp.transpose(y.reshape(t_s, n_s, width), (1, 0, 2)).reshape(n_s * t_s, width)
    y32 = jnp.concatenate([yp32.reshape(mp, width), back(ys32)], axis=0)
    y16 = jnp.concatenate([yp16.reshape(mp, width), back(ys16)], axis=0)
    gated = matmul(y16, w_glu.astype(BF16), out_dtype=BF16, glu_y=y32)
    x = matmul(gated, w_out.astype(BF16), resid=x)
    states = (spr.reshape(n_p, g, p), spi.reshape(n_p, g, p), ssr.reshape(n_s, g, p), ssi.reshape(n_s, g, p))
    return x, states


def _ffn_layer(x, hn, dims, state, w_up, w_dw, b_dw, w_down):
    n_p, t_p, n_s, t_s = dims
    f = w_down.shape[0]
    tm = _pick(math.gcd(t_p, n_s * t_s), 512)
    e = jnp.pad(state, ((0, 0), (t_s - (CONV_W - 1), 0), (0, 0))).reshape(n_s * t_s, 2 * f)
    w16 = w_up.astype(BF16)
    hid, tail_g, tail_v, up_g, up_v = ffn_up(
        hn, w16[:, :f], w16[:, f:], w_dw[:, :f], w_dw[:, f:], b_dw[:f].reshape(1, f), b_dw[f:].reshape(1, f),
        e[:, :f], e[:, f:], n_p, t_p, t_s, tm, 512)
    x = matmul(hid, w_down.astype(BF16), resid=x)
    keep = CONV_W - 1
    conv_p = jnp.concatenate([tail_g[:, SUBLANES - keep:], tail_v[:, SUBLANES - keep:]], axis=-1)
    conv_s = jnp.concatenate([up_g.reshape(n_s, t_s, f)[:, t_s - keep:], up_v.reshape(n_s, t_s, f)[:, t_s - keep:]],
                             axis=-1)
    return x, conv_p, conv_s


def kernel(x_prompt, x_sample, cache_sb_k, cache_sb_v, cache_mla_ckv, cache_mla_kpe, page_table, state_ssm_re, state_ssm_im, state_ffn_conv, norm_mix, norm_ffn, norm_final, attn_w_in, attn_g_q, attn_g_kv, attn_w_uq, attn_w_ukv, attn_w_out, ssm_w_in, ssm_a_re, ssm_a_im, ssm_log_step, ssm_b_re, ssm_b_im, ssm_c_re, ssm_c_im, ssm_d, ssm_w_glu, ssm_w_out, ffn_w_up, ffn_w_dw, ffn_b_dw, ffn_w_down):
    n_p, t_p, d = x_prompt.shape
    n_s, t_s, _ = x_sample.shape
    dims = (n_p, t_p, n_s, t_s)
    mp, ms = n_p * t_p, n_s * t_s
    depth = norm_mix.shape[0]
    past_len = page_table.shape[1] * cache_sb_k.shape[2]
    x = jnp.concatenate([x_prompt.reshape(mp, d), x_sample.reshape(ms, d)], axis=0)
    pos = jnp.concatenate([jnp.tile(jnp.arange(t_p, dtype=jnp.int32), n_p),
                           jnp.tile(past_len + jnp.arange(t_s, dtype=jnp.int32), n_s)])
    cos_t, sin_t = _rope_tables(pos)
    tri_n = 2 * LANES
    tri = (lax.broadcasted_iota(jnp.int32, (tri_n, tri_n), 0)
           > lax.broadcasted_iota(jnp.int32, (tri_n, tri_n), 1)).astype(BF16)
    caches = (cache_sb_k, cache_sb_v, cache_mla_ckv, cache_mla_kpe)

    rows_p, rows_s, ssm_p, ssm_s, conv_p, conv_s = [], [], [], [], [], []
    for layer in range(depth):
        i = layer // 2
        hn = rmsnorm_rows(x, norm_mix[layer], BF16)
        if layer % 2 == 0:
            x, rows = _attention_layer(x, hn, i, dims, cos_t, sin_t, tri, page_table, caches,
                                       attn_w_in[i], attn_g_q[i], attn_g_kv[i], attn_w_uq[i], attn_w_ukv[i],
                                       attn_w_out[i])
            rows_p.append([r[:mp] for r in rows])
            rows_s.append([r[mp:] for r in rows])
        else:
            x, st = _ssm_layer(x, hn, dims, (state_ssm_re[i], state_ssm_im[i]), ssm_w_in[i], ssm_a_re[i],
                               ssm_a_im[i], ssm_log_step[i], ssm_b_re[i], ssm_b_im[i], ssm_c_re[i], ssm_c_im[i],
                               ssm_d[i], ssm_w_glu[i], ssm_w_out[i])
            ssm_p.append(st[:2])
            ssm_s.append(st[2:])
        hn = rmsnorm_rows(x, norm_ffn[layer], BF16)
        x, cp, cs = _ffn_layer(x, hn, dims, state_ffn_conv[layer], ffn_w_up[layer], ffn_w_dw[layer],
                               ffn_b_dw[layer], ffn_w_down[layer])
        conv_p.append(cp)
        conv_s.append(cs)

    y_prompt = rmsnorm_rows(x, norm_final, F32, 0, mp).reshape(n_p, t_p, d)
    y_sample = rmsnorm_rows(x, norm_final, F32, mp, ms).reshape(n_s, t_s, d)
    kvs = (SB_KV_HEADS, SB_HEAD_DIM)
    stack = lambda rows, k, shape: jnp.stack([r[k].reshape(shape) for r in rows])
    return (y_prompt, y_sample,
            stack(rows_p, 0, (n_p, t_p) + kvs), stack(rows_p, 1, (n_p, t_p) + kvs),
            stack(rows_p, 2, (n_p, t_p, -1)), stack(rows_p, 3, (n_p, t_p, -1)),
            jnp.stack([s[0] for s in ssm_p]), jnp.stack([s[1] for s in ssm_p]), jnp.stack(conv_p),
            stack(rows_s, 0, (n_s, t_s) + kvs), stack(rows_s, 1, (n_s, t_s) + kvs),
            stack(rows_s, 2, (n_s, t_s, -1)), stack(rows_s, 3, (n_s, t_s, -1)),
            jnp.stack([s[0] for s in ssm_s]), jnp.stack([s[1] for s in ssm_s]), jnp.stack(conv_s))
```

```python
import functools
import math

import jax
import jax.numpy as jnp
from jax import lax
from jax.experimental import pallas as pl
from jax.experimental.pallas import tpu as pltpu

F32 = jnp.float32
BF16 = jnp.bfloat16
EPS = 1e-6

SB_KV_HEADS = 2
SB_REP = 4
SB_HEAD_DIM = 128
SB_SCALE = SB_HEAD_DIM ** -0.5
MLA_HEADS = 8
MLA_Q_LORA = 768
MLA_KV_LORA = 512
MLA_NOPE = 128
MLA_ROPE = 64
MLA_V = 128
MLA_SCALE = (MLA_NOPE + MLA_ROPE) ** -0.5
ROPE_BASE = 10000.0
SSM_GROUP = 16
SSM_STATE = 64
CONV_W = 3

V7X_VMEM_BYTES = 64 * 2 ** 20
VMEM_LIMIT_BYTES = V7X_VMEM_BYTES - 8 * 2 ** 20
LANES = 128
SUBLANES = 8

SB_QW = SB_KV_HEADS * SB_REP * SB_HEAD_DIM
SB_KW = SB_KV_HEADS * SB_HEAD_DIM
KC_W = MLA_KV_LORA + LANES
NEG_BIG = -1e30


def _cparams(*sem):
    return pltpu.CompilerParams(dimension_semantics=sem, vmem_limit_bytes=VMEM_LIMIT_BYTES)


def _dot(a, b):
    return jnp.dot(a, b, preferred_element_type=F32)


def _dot_nt(a, b):
    return lax.dot_general(a, b, (((1,), (1,)), ((), ())), preferred_element_type=F32)


def _pick(n, pref):
    t = min(n, pref)
    while n % t:
        t //= 2
    return t


def _rmsnorm_body(x_ref, g_ref, o_ref):
    x = x_ref[...]
    y = x * lax.rsqrt(jnp.mean(x * x, axis=-1, keepdims=True) + EPS)
    o_ref[...] = (y * g_ref[...]).astype(o_ref.dtype)


def rmsnorm_rows(x, g, out_dtype, row_start=0, n_rows=None, tm=512):
    m, d = x.shape
    n_rows = m - row_start if n_rows is None else n_rows
    tm = _pick(math.gcd(n_rows, row_start) if row_start else n_rows, tm)
    off = row_start // tm
    return pl.pallas_call(
        _rmsnorm_body,
        grid=(n_rows // tm,),
        in_specs=[pl.BlockSpec((tm, d), lambda i: (i + off, 0)),
                  pl.BlockSpec((1, d), lambda i: (0, 0))],
        out_specs=pl.BlockSpec((tm, d), lambda i: (i, 0)),
        out_shape=jax.ShapeDtypeStruct((n_rows, d), out_dtype),
        compiler_params=_cparams("parallel"),
        name="rmsnorm",
    )(x, g.reshape(1, d).astype(F32))


def _mm_body(a_ref, w_ref, o_ref):
    o_ref[...] = _dot(a_ref[...], w_ref[...]).astype(o_ref.dtype)


def _mm_resid_body(a_ref, w_ref, r_ref, o_ref):
    o_ref[...] = r_ref[...] + _dot(a_ref[...], w_ref[...])


def _glu_body(a_ref, y_ref, w_ref, o_ref):
    y = y_ref[...]
    gate = 1.0 / (1.0 + jnp.exp(-_dot(a_ref[...], w_ref[...])))
    o_ref[...] = (y * gate).astype(o_ref.dtype)


def matmul(a, w, layer, out_dtype=F32, resid=None, glu_y=None, tm=512, tn=512):
    m, k = a.shape
    n = w.shape[2]
    tm, tn = _pick(m, tm), _pick(n, tn)
    a_spec = pl.BlockSpec((tm, k), lambda i, j: (i, 0))
    w_spec = pl.BlockSpec((None, k, tn), lambda i, j: (layer, 0, j))
    t_spec = pl.BlockSpec((tm, tn), lambda i, j: (i, j))
    if resid is not None:
        body, specs, args = _mm_resid_body, [a_spec, w_spec, t_spec], (a, w, resid)
    elif glu_y is not None:
        body, specs, args = _glu_body, [a_spec, t_spec, w_spec], (a, glu_y, w)
    else:
        body, specs, args = _mm_body, [a_spec, w_spec], (a, w)
    return pl.pallas_call(
        body,
        grid=(m // tm, n // tn),
        in_specs=specs,
        out_specs=t_spec,
        out_shape=jax.ShapeDtypeStruct((m, n), out_dtype),
        compiler_params=_cparams("parallel", "arbitrary"),
        name="matmul",
    )(*args)


_O_K = SB_QW
_O_V = _O_K + SB_KW
_O_CQ = _O_V + SB_KW
_O_CKV = _O_CQ + MLA_Q_LORA
_O_PE = _O_CKV + MLA_KV_LORA
_O_PESW = _O_PE + LANES
ATT_IN_EXT = _O_PESW + LANES


def _attn_in_body(h_ref, w_ref, gq_ref, gkv_ref, cos_ref, sin_ref,
                  qsb_ref, k_ref, v_ref, cq_ref, ckv_ref, kpe_ref, kc_ref):
    h = h_ref[...]

    def seg(lo, hi):
        return _dot(h, w_ref[:, lo:hi])

    def norm(x, g_ref):
        return x * lax.rsqrt(jnp.mean(x * x, axis=-1, keepdims=True) + EPS) * g_ref[...]

    qsb_ref[...] = seg(0, _O_K).astype(qsb_ref.dtype)
    k_ref[...] = seg(_O_K, _O_V)
    v_ref[...] = seg(_O_V, _O_CQ)
    cq_ref[...] = norm(seg(_O_CQ, _O_CKV), gq_ref).astype(cq_ref.dtype)
    ckv = norm(seg(_O_CKV, _O_PE), gkv_ref)
    ckv_ref[...] = ckv
    kpe = seg(_O_PE, _O_PESW) * cos_ref[...] + seg(_O_PESW, ATT_IN_EXT) * sin_ref[...]
    kpe_ref[...] = kpe[:, :MLA_ROPE]
    kc_ref[:, :MLA_KV_LORA] = ckv.astype(kc_ref.dtype)
    kc_ref[:, MLA_KV_LORA:] = kpe.astype(kc_ref.dtype)


def attn_in(h, w_ext, g_q, g_kv, cos_t, sin_t, tm=256):
    m, d = h.shape
    tm = _pick(m, tm)
    row = lambda w: pl.BlockSpec((tm, w), lambda i: (i, 0))
    full = lambda a: pl.BlockSpec(a.shape, lambda i: (0,) * a.ndim)
    outs = [(SB_QW, BF16), (SB_KW, F32), (SB_KW, F32), (MLA_Q_LORA, BF16),
            (MLA_KV_LORA, F32), (MLA_ROPE, F32), (KC_W, BF16)]
    return pl.pallas_call(
        _attn_in_body,
        grid=(m // tm,),
        in_specs=[row(d), full(w_ext), full(g_q), full(g_kv), row(LANES), row(LANES)],
        out_specs=[row(w) for w, _ in outs],
        out_shape=[jax.ShapeDtypeStruct((m, w), dt) for w, dt in outs],
        compiler_params=_cparams("parallel"),
        name="attn_in",
    )(h, w_ext, g_q, g_kv, cos_t, sin_t)


def _mla_q_body(cq_ref, wuq_ref, wuk_ref, cos_ref, sin_ref, o_ref):
    cq = cq_ref[...]
    hw = MLA_HEADS * LANES
    q_nope = _dot(cq, wuq_ref[:, :hw]).astype(BF16)
    pe = _dot(cq, wuq_ref[:, hw:2 * hw])
    pe_sw = _dot(cq, wuq_ref[:, 2 * hw:])
    cos = cos_ref[...]
    sin = sin_ref[...]
    for h in range(MLA_HEADS):
        sl = slice(h * LANES, (h + 1) * LANES)
        o_ref[h, :, :MLA_KV_LORA] = _dot(q_nope[:, sl], wuk_ref[h]).astype(o_ref.dtype)
        o_ref[h, :, MLA_KV_LORA:] = (pe[:, sl] * cos + pe_sw[:, sl] * sin).astype(o_ref.dtype)


def mla_q(cqn, wuq_ext, wuk_t, cos_t, sin_t, tm=256):
    m = cqn.shape[0]
    tm = _pick(m, tm)
    row = lambda w: pl.BlockSpec((tm, w), lambda i: (i, 0))
    full = lambda a: pl.BlockSpec(a.shape, lambda i: (0,) * a.ndim)
    return pl.pallas_call(
        _mla_q_body,
        grid=(m // tm,),
        in_specs=[row(MLA_Q_LORA), full(wuq_ext), full(wuk_t), row(LANES), row(LANES)],
        out_specs=pl.BlockSpec((MLA_HEADS, tm, KC_W), lambda i: (0, i, 0)),
        out_shape=jax.ShapeDtypeStruct((MLA_HEADS, m, KC_W), BF16),
        compiler_params=_cparams("parallel"),
        name="mla_q",
    )(cqn, wuq_ext, wuk_t, cos_t, sin_t)


def _split3(x):
    hi = x.astype(BF16)
    r = x - hi.astype(F32)
    mid = r.astype(BF16)
    lo = (r - mid.astype(F32)).astype(BF16)
    return hi, mid, lo


def _later_sum(ls, tri):
    hi, mid, lo = _split3(ls)
    return _dot(hi, tri) + _dot(mid, tri) + _dot(lo, tri)


def _stick_weights(z, vis, carry, tri):
    soft = jnp.log(1.0 + jnp.exp(-jnp.abs(z)))
    log_beta = jnp.minimum(z, 0.0) - soft
    log_stay = log_beta - z
    if vis is not None:
        log_stay = jnp.where(vis, log_stay, 0.0)
    w = jnp.exp(log_beta + _later_sum(log_stay, tri) + carry)
    if vis is not None:
        w = jnp.where(vis, w, 0.0)
    return w, carry + jnp.sum(log_stay, axis=-1, keepdims=True)


def _sb_prompt_body(q_ref, k_ref, v_ref, tri_ref, o_ref, *, tq):
    qi = pl.program_id(2)
    q = q_ref[...]
    qs = jnp.concatenate([q[:, r * SB_HEAD_DIM:(r + 1) * SB_HEAD_DIM] for r in range(SB_REP)], axis=0)
    rows = SB_REP * tq
    q_pos = qi * tq + lax.broadcasted_iota(jnp.int32, (rows, tq), 0) % tq
    k_off = lax.broadcasted_iota(jnp.int32, (rows, tq), 1)
    tri = tri_ref[...]

    def step(i, state):
        acc, carry = state
        kb = qi - i
        start = pl.multiple_of(kb * tq, tq)
        k = k_ref[pl.ds(start, tq), :].astype(BF16)
        v = v_ref[pl.ds(start, tq), :].astype(BF16)
        z = _dot_nt(qs, k) * SB_SCALE
        vis = (kb * tq + k_off) < q_pos
        w, carry = _stick_weights(z, vis, carry, tri)
        return acc + _dot(w.astype(BF16), v), carry

    acc, _ = lax.fori_loop(0, qi + 1, step,
                           (jnp.zeros((rows, SB_HEAD_DIM), F32), jnp.zeros((rows, 1), F32)))
    for r in range(SB_REP):
        o_ref[:, r * SB_HEAD_DIM:(r + 1) * SB_HEAD_DIM] = acc[r * tq:(r + 1) * tq].astype(o_ref.dtype)


def sb_prompt_attention(q_sb, k_sb, v_sb, tri, n_seq, seq_len, tq):
    nq = seq_len // tq
    gw = SB_REP * SB_HEAD_DIM
    return pl.pallas_call(
        functools.partial(_sb_prompt_body, tq=tq),
        grid=(n_seq, SB_KV_HEADS, nq),
        in_specs=[pl.BlockSpec((tq, gw), lambda b, g, i: (b * nq + i, g)),
                  pl.BlockSpec((seq_len, SB_HEAD_DIM), lambda b, g, i: (b, g)),
                  pl.BlockSpec((seq_len, SB_HEAD_DIM), lambda b, g, i: (b, g)),
                  pl.BlockSpec(tri.shape, lambda b, g, i: (0, 0))],
        out_specs=pl.BlockSpec((tq, gw), lambda b, g, i: (b * nq + i, g)),
        out_shape=jax.ShapeDtypeStruct((n_seq * seq_len, SB_QW), BF16),
        compiler_params=_cparams("parallel", "parallel", "arbitrary"),
        name="sb_prompt_attention",
    )(q_sb, k_sb, v_sb, tri)


def _mla_prompt_body(q_ref, kc_ref, wuv_ref, o_ref, m_ref, l_ref, acc_ref, *, tq, tk):
    qi = pl.program_id(1)
    rows = MLA_HEADS * tq
    q = q_ref[...].reshape(rows, KC_W)
    q_pos = qi * tq + lax.broadcasted_iota(jnp.int32, (rows, tk), 0) % tq
    k_off = lax.broadcasted_iota(jnp.int32, (rows, tk), 1)
    m_ref[...] = jnp.full(m_ref.shape, NEG_BIG, F32)
    l_ref[...] = jnp.zeros(l_ref.shape, F32)
    acc_ref[...] = jnp.zeros(acc_ref.shape, F32)

    def step(kb, _):
        start = pl.multiple_of(kb * tk, tk)
        kc = kc_ref[pl.ds(start, tk), :]
        s = _dot_nt(q, kc) * MLA_SCALE
        s = jnp.where((kb * tk + k_off) <= q_pos, s, -jnp.inf)
        m_old = m_ref[...]
        m_new = jnp.maximum(m_old, jnp.max(s, axis=-1, keepdims=True))
        alpha = jnp.exp(m_old - m_new)
        p = jnp.exp(s - m_new)
        l_ref[...] = alpha * l_ref[...] + jnp.sum(p, axis=-1, keepdims=True)
        acc_ref[...] = alpha * acc_ref[...] + _dot(p.astype(BF16), kc[:, :MLA_KV_LORA])
        m_ref[...] = m_new
        return 0

    n_kb = ((qi + 1) * tq + tk - 1) // tk
    lax.fori_loop(0, n_kb, step, 0)
    o_lat = (acc_ref[...] / l_ref[...]).astype(BF16)
    for h in range(MLA_HEADS):
        o_ref[:, h * MLA_V:(h + 1) * MLA_V] = _dot(o_lat[h * tq:(h + 1) * tq], wuv_ref[h]).astype(o_ref.dtype)


def mla_prompt_attention(qcat, kc, wuv, n_seq, seq_len, tq, tk):
    nq = seq_len // tq
    rows = MLA_HEADS * tq
    return pl.pallas_call(
        functools.partial(_mla_prompt_body, tq=tq, tk=tk),
        grid=(n_seq, nq),
        in_specs=[pl.BlockSpec((MLA_HEADS, tq, KC_W), lambda b, i: (0, b * nq + i, 0)),
                  pl.BlockSpec((seq_len, KC_W), lambda b, i: (b, 0)),
                  pl.BlockSpec(wuv.shape, lambda b, i: (0, 0, 0))],
        out_specs=pl.BlockSpec((tq, MLA_HEADS * MLA_V), lambda b, i: (b * nq + i, 0)),
        out_shape=jax.ShapeDtypeStruct((n_seq * seq_len, MLA_HEADS * MLA_V), BF16),
        scratch_shapes=[pltpu.VMEM((rows, 1), F32), pltpu.VMEM((rows, 1), F32),
                        pltpu.VMEM((rows, MLA_KV_LORA), F32)],
        compiler_params=_cparams("parallel", "arbitrary"),
        name="mla_prompt_attention",
    )(qcat, kc, wuv)


def _decode_body(pt_ref, qsb_ref, qa_ref, qp_ref, nk_ref, nv_ref, nc_ref, nr_ref, wuv_ref, tri_ref,
                 ck_hbm, cv_hbm, cc_hbm, cr_hbm,
                 osb_ref, omla_ref,
                 kbuf, vbuf, cbuf, rbuf, sems, padk, padv, padc, padr,
                 sbacc_ref, carry_ref, m_ref, l_ref, acc_ref,
                 *, layer, n_seq, n_pages, ppc, page, t_new):
    b = pl.program_id(0)
    c = pl.program_id(1)
    n_chunks = n_pages // ppc
    n = b * n_chunks + c
    slot = n % 2
    hbm = (ck_hbm, cv_hbm, cc_hbm, cr_hbm)
    bufs = (kbuf, vbuf, cbuf, rbuf)
    qrows = SB_KV_HEADS * SB_REP * t_new
    hr = SB_REP * t_new
    kvrows = SB_KV_HEADS * page

    def page_copies(bb, cc, sl):
        first = bb * n_pages + (n_chunks - 1 - cc) * ppc
        out = []
        for j in range(ppc):
            pid = pt_ref[first + j]
            for a in range(4):
                out.append(pltpu.make_async_copy(hbm[a].at[layer, pid], bufs[a].at[sl, j], sems.at[sl, a]))
        return out

    @pl.when(n == 0)
    def _():
        for cp in page_copies(0, 0, 0):
            cp.start()
        for ref in (padk, padv, padc, padr):
            ref[...] = jnp.zeros(ref.shape, F32)

    @pl.when(n + 1 < n_seq * n_chunks)
    def _():
        nxt = n + 1
        for cp in page_copies(nxt // n_chunks, nxt % n_chunks, 1 - slot):
            cp.start()

    q_sb = qsb_ref[0].astype(F32)
    q_a = qa_ref[0].astype(F32)
    q_p = qp_ref[0].astype(F32)
    tri = tri_ref[...]
    own_head = (lax.broadcasted_iota(jnp.int32, (qrows, kvrows), 1) % SB_KV_HEADS
                == lax.broadcasted_iota(jnp.int32, (qrows, kvrows), 0) // hr)

    def attend(k, v, lat, rope_t, sb_vis, mla_vis):
        n_pg = len(rope_t)
        z = _dot_nt(q_sb, k) * SB_SCALE
        log_beta, within, total = [], [], []
        for j in range(n_pg):
            zj = z[:, j * kvrows:(j + 1) * kvrows]
            soft = jnp.log(1.0 + jnp.exp(-jnp.abs(zj)))
            lb = jnp.minimum(zj, 0.0) - soft
            log_stay = jnp.where(sb_vis, lb - zj, 0.0)
            log_beta.append(lb)
            within.append(_later_sum(log_stay, tri))
            total.append(jnp.sum(log_stay, axis=-1, keepdims=True))
        carry = carry_ref[...]
        w = [None] * n_pg
        for j in range(n_pg - 1, -1, -1):
            w[j] = jnp.where(sb_vis, jnp.exp(log_beta[j] + within[j] + carry), 0.0)
            carry = carry + total[j]
        carry_ref[...] = carry
        sbacc_ref[...] += _dot(w[0] if n_pg == 1 else jnp.concatenate(w, axis=1), v)
        s_rope = [_dot(q_p, r) for r in rope_t]
        s = (_dot_nt(q_a, lat) + (s_rope[0] if n_pg == 1 else jnp.concatenate(s_rope, axis=1))) * MLA_SCALE
        if mla_vis is not None:
            s = jnp.where(mla_vis, s, -jnp.inf)
        m_old = m_ref[...]
        m_new = jnp.maximum(m_old, jnp.max(s, axis=-1, keepdims=True))
        alpha = jnp.exp(m_old - m_new)
        p = jnp.exp(s - m_new)
        l_ref[...] = alpha * l_ref[...] + jnp.sum(p, axis=-1, keepdims=True)
        acc_ref[...] = alpha * acc_ref[...] + _dot(p, lat)
        m_ref[...] = m_new

    @pl.when(c == 0)
    def _():
        sbacc_ref[...] = jnp.zeros(sbacc_ref.shape, F32)
        carry_ref[...] = jnp.zeros(carry_ref.shape, F32)
        m_ref[...] = jnp.full(m_ref.shape, NEG_BIG, F32)
        l_ref[...] = jnp.zeros(l_ref.shape, F32)
        acc_ref[...] = jnp.zeros(acc_ref.shape, F32)
        padk[:SB_KV_HEADS * t_new] = nk_ref[0]
        padv[:SB_KV_HEADS * t_new] = nv_ref[0]
        padc[:t_new] = nc_ref[0]
        padr[:, :t_new] = nr_ref[0]
        tok_sb = lax.broadcasted_iota(jnp.int32, (qrows, kvrows), 0) % t_new
        key_sb = lax.broadcasted_iota(jnp.int32, (qrows, kvrows), 1) // SB_KV_HEADS
        tok = lax.broadcasted_iota(jnp.int32, (qrows, page), 0) % t_new
        key = lax.broadcasted_iota(jnp.int32, (qrows, page), 1)
        attend(padk[...], padv[...], padc[...], [padr[...]], own_head & (key_sb < tok_sb), key <= tok)

    for cp in page_copies(b, c, slot):
        cp.wait()

    attend(kbuf[slot].reshape(ppc * kvrows, SB_HEAD_DIM), vbuf[slot].reshape(ppc * kvrows, SB_HEAD_DIM),
           cbuf[slot].reshape(ppc * page, MLA_KV_LORA), [rbuf[slot, j] for j in range(ppc)], own_head, None)

    @pl.when(c == n_chunks - 1)
    def _():
        osb_ref[0] = sbacc_ref[...].astype(osb_ref.dtype)
        o_lat = (acc_ref[...] / l_ref[...]).astype(BF16)
        omla_ref[0] = jnp.concatenate(
            [_dot(o_lat[h * t_new:(h + 1) * t_new], wuv_ref[h]) for h in range(MLA_HEADS)],
            axis=0).astype(omla_ref.dtype)


def decode_attention(page_table, q_sb, q_a, q_p, new_k, new_v, new_c, new_r, wuv, tri,
                     cache_k, cache_v, cache_c, cache_r, layer, ppc):
    n_seq, n_pages = page_table.shape
    t_new = new_c.shape[1]
    page = cache_c.shape[2]
    kvrows = SB_KV_HEADS * page
    assert tri.shape[0] == kvrows and n_pages % ppc == 0 and t_new <= page
    assert SB_KV_HEADS * SB_REP == MLA_HEADS
    n_chunks = n_pages // ppc
    qrows = MLA_HEADS * t_new
    blk = lambda a: pl.BlockSpec((1,) + a.shape[1:], lambda b, c, pt: (b,) + (0,) * (a.ndim - 1))
    full = lambda a: pl.BlockSpec(a.shape, lambda b, c, pt: (0,) * a.ndim)
    any_spec = pl.BlockSpec(memory_space=pl.ANY)
    grid_spec = pltpu.PrefetchScalarGridSpec(
        num_scalar_prefetch=1,
        grid=(n_seq, n_chunks),
        in_specs=[blk(q_sb), blk(q_a), blk(q_p), blk(new_k), blk(new_v), blk(new_c), blk(new_r),
                  full(wuv), full(tri), any_spec, any_spec, any_spec, any_spec],
        out_specs=[pl.BlockSpec((1, qrows, SB_HEAD_DIM), lambda b, c, pt: (b, 0, 0)),
                   pl.BlockSpec((1, qrows, MLA_V), lambda b, c, pt: (b, 0, 0))],
        scratch_shapes=[pltpu.VMEM((2, ppc, kvrows, SB_HEAD_DIM), F32), pltpu.VMEM((2, ppc, kvrows, SB_HEAD_DIM), F32),
                        pltpu.VMEM((2, ppc, page, MLA_KV_LORA), F32), pltpu.VMEM((2, ppc, MLA_ROPE, page), F32),
                        pltpu.SemaphoreType.DMA((2, 4)),
                        pltpu.VMEM((kvrows, SB_HEAD_DIM), F32), pltpu.VMEM((kvrows, SB_HEAD_DIM), F32),
                        pltpu.VMEM((page, MLA_KV_LORA), F32), pltpu.VMEM((MLA_ROPE, page), F32),
                        pltpu.VMEM((qrows, SB_HEAD_DIM), F32), pltpu.VMEM((qrows, 1), F32),
                        pltpu.VMEM((qrows, 1), F32), pltpu.VMEM((qrows, 1), F32),
                        pltpu.VMEM((qrows, MLA_KV_LORA), F32)],
    )
    return pl.pallas_call(
        functools.partial(_decode_body, layer=layer, n_seq=n_seq, n_pages=n_pages, ppc=ppc,
                          page=page, t_new=t_new),
        grid_spec=grid_spec,
        out_shape=[jax.ShapeDtypeStruct((n_seq, qrows, SB_HEAD_DIM), BF16),
                   jax.ShapeDtypeStruct((n_seq, qrows, MLA_V), BF16)],
        compiler_params=_cparams("arbitrary", "arbitrary"),
        name="decode_attention",
    )(page_table.reshape(-1), q_sb, q_a, q_p, new_k, new_v, new_c, new_r, wuv, tri,
      cache_k, cache_v, cache_c, cache_r)


def _zoh_body(are_ref, aim_ref, step_ref, bre_ref, bim_ref, lr_ref, li_ref, br_ref, bi_ref):
    dt = jnp.exp(step_ref[...])
    ar = are_ref[...]
    ai = aim_ref[...]
    mag = jnp.exp(ar * dt)
    lam_r = mag * jnp.cos(ai * dt)
    lam_i = mag * jnp.sin(ai * dt)
    den = ar * ar + ai * ai
    nr = lam_r - 1.0
    fr = (nr * ar + lam_i * ai) / den
    fi = (lam_i * ar - nr * ai) / den
    lr_ref[...] = lam_r
    li_ref[...] = lam_i
    for h in range(SSM_GROUP):
        br = bre_ref[h]
        bi = bim_ref[h]
        br_ref[h] = fr * br - fi * bi
        bi_ref[h] = fr * bi + fi * br


def zoh(a_re, a_im, log_step, b_re, b_im):
    g, p = a_re.shape
    hgp = jax.ShapeDtypeStruct((SSM_GROUP, g, p), F32)
    gp = jax.ShapeDtypeStruct((g, p), F32)
    return pl.pallas_call(_zoh_body, out_shape=[gp, gp, hgp, hgp], name="ssm_zoh")(
        a_re, a_im, log_step.reshape(g, 1), jnp.transpose(b_re, (2, 0, 1)), jnp.transpose(b_im, (2, 0, 1)))


def _gelu_tanh(y):
    return 0.5 * y * (1.0 + jnp.tanh(math.sqrt(2.0 / math.pi) * (y + 0.044715 * (y * y * y))))


def _ssm_body(u_ref, bre_ref, bim_ref, cre_ref, cim_ref, d_ref, lr_ref, li_ref, h0r_ref, h0i_ref,
              y32_ref, y16_ref, sr_ref, si_ref, xr_s, xi_s, *, nb, steps, gb, cw, sw):
    tc = pl.program_id(2)
    for k in range(gb):
        u16 = u_ref[:, k * cw:(k + 1) * cw].astype(BF16)
        xr_s[k] = _dot(u16, bre_ref[k])
        xi_s[k] = _dot(u16, bim_ref[k])
    lam = [(lr_ref[k], li_ref[k]) for k in range(gb)]

    @pl.when(tc == 0)
    def _():
        sr_ref[0] = h0r_ref[0]
        si_ref[0] = h0i_ref[0]

    def step(t, state):
        rows = pl.ds(pl.multiple_of(t * nb, nb), nb)
        out = []
        for k in range(gb):
            sr, si = state[2 * k], state[2 * k + 1]
            lam_r, lam_i = lam[k]
            nr = lam_r * sr - lam_i * si + xr_s[k, rows, :]
            ni = lam_r * si + lam_i * sr + xi_s[k, rows, :]
            xr_s[k, rows, :] = nr
            xi_s[k, rows, :] = ni
            out += [nr, ni]
        return tuple(out)

    init = []
    for k in range(gb):
        init += [sr_ref[0, :, k * sw:(k + 1) * sw], si_ref[0, :, k * sw:(k + 1) * sw]]
    final = lax.fori_loop(0, steps, step, tuple(init))
    for k in range(gb):
        sr_ref[0, :, k * sw:(k + 1) * sw] = final[2 * k]
        si_ref[0, :, k * sw:(k + 1) * sw] = final[2 * k + 1]
        cs = slice(k * cw, (k + 1) * cw)
        y = _dot(xr_s[k].astype(BF16), cre_ref[k]) - _dot(xi_s[k].astype(BF16), cim_ref[k])
        y = _gelu_tanh(y + d_ref[:, cs] * u_ref[:, cs])
        y32_ref[:, cs] = y
        y16_ref[:, cs] = y.astype(BF16)


def ssm_scan(u, group_rows, bre, bim, cre, cim, d, lam_r, lam_i, h0r, h0i, nb, steps, gb):
    w = u.shape[1]
    n_blk, cw, sw = bre.shape
    n_grp = h0r.shape[0]
    tr = steps * nb
    assert n_blk % gb == 0 and group_rows % tr == 0
    n_t = group_rows // tr
    rows = n_grp * group_rows
    act = pl.BlockSpec((tr, gb * cw), lambda s, j, t: (s * n_t + t, j))
    par = lambda a, b: pl.BlockSpec((gb, a, b), lambda s, j, t: (j, 0, 0))
    st = pl.BlockSpec((1, nb, gb * sw), lambda s, j, t: (s, 0, j))
    return pl.pallas_call(
        functools.partial(_ssm_body, nb=nb, steps=steps, gb=gb, cw=cw, sw=sw),
        grid=(n_grp, n_blk // gb, n_t),
        in_specs=[act, par(cw, sw), par(cw, sw), par(sw, cw), par(sw, cw),
                  pl.BlockSpec((1, gb * cw), lambda s, j, t: (0, j)), par(1, sw), par(1, sw), st, st],
        out_specs=[act, act, st, st],
        out_shape=[jax.ShapeDtypeStruct((rows, w), F32), jax.ShapeDtypeStruct((rows, w), BF16),
                   jax.ShapeDtypeStruct(h0r.shape, F32), jax.ShapeDtypeStruct(h0i.shape, F32)],
        scratch_shapes=[pltpu.VMEM((gb, tr, sw), F32), pltpu.VMEM((gb, tr, sw), F32)],
        compiler_params=_cparams("parallel", "parallel", "arbitrary"),
        name="ssm_scan",
    )(u, bre, bim, cre, cim, d, lam_r, lam_i, h0r, h0i)


def _ffn_up_body(h_ref, wg_ref, wv_ref, dwg_ref, dwv_ref, bg_ref, bv_ref, eg_ref, ev_ref,
                 hid_ref, tailg_ref, tailv_ref, upg_ref, upv_ref, ug_ref, uv_ref,
                 *, tm, tn, csub, n_prompt_tiles, tiles_per_seq, t_new):
    i = pl.program_id(1)
    h = h_ref[...]
    lo = SUBLANES
    sides = ((wg_ref, ug_ref, dwg_ref, bg_ref, eg_ref), (wv_ref, uv_ref, dwv_ref, bv_ref, ev_ref))

    def tile(sample):
        rt = lax.broadcasted_iota(jnp.int32, (tm, 1), 0) % t_new
        for c0 in range(0, tn, csub):
            cs = slice(c0, c0 + csub)
            mixed = []
            for w_ref, u_ref, dw_ref, b_ref, e_ref in sides:
                u_ref[lo:, cs] = _dot(h, w_ref[:, cs])
                m1 = u_ref[lo - 1:lo - 1 + tm, cs]
                m2 = u_ref[lo - 2:lo - 2 + tm, cs]
                if sample:
                    e = e_ref[:, cs]
                    m1 = jnp.where(rt >= 1, m1, pltpu.roll(e, tm - (t_new - 1), 0))
                    m2 = jnp.where(rt >= 2, m2, pltpu.roll(e, tm - (t_new - 2), 0))
                mixed.append(((b_ref[:, cs] + dw_ref[0:1, cs] * m2) + dw_ref[1:2, cs] * m1)
                             + dw_ref[2:3, cs] * u_ref[lo:, cs])
            gate, val = mixed
            hid_ref[:, cs] = (gate * (1.0 / (1.0 + jnp.exp(-gate))) * val).astype(hid_ref.dtype)

    @pl.when(i < n_prompt_tiles)
    def _():
        @pl.when(i % tiles_per_seq == 0)
        def _():
            ug_ref[:lo] = jnp.zeros((lo, tn), F32)
            uv_ref[:lo] = jnp.zeros((lo, tn), F32)

        tile(False)
        for u_ref, tail_ref in ((ug_ref, tailg_ref), (uv_ref, tailv_ref)):
            last = u_ref[tm:]
            tail_ref[0] = last
            u_ref[:lo] = last

    @pl.when(i >= n_prompt_tiles)
    def _():
        tile(True)
        upg_ref[...] = ug_ref[lo:]
        upv_ref[...] = uv_ref[lo:]


def ffn_up(h, w_up, w_dw, b_dw, e, layer, n_prompt_seq, seq_len, t_new, tm, tn):
    m, d = h.shape
    f = w_up.shape[2] // 2
    n_prompt_tiles = n_prompt_seq * seq_len // tm
    tiles_per_seq = seq_len // tm
    ms = m - n_prompt_seq * seq_len
    n_sample_tiles = ms // tm
    assert seq_len % tm == 0 and ms % tm == 0 and tm % t_new == 0 and t_new >= CONV_W - 1
    assert n_prompt_tiles > 0 and CONV_W - 1 <= SUBLANES
    tn = _pick(f, tn)
    csub = _pick(tn, 2 * LANES)
    nj = f // tn
    last_seq = n_prompt_seq - 1
    par = lambda rows, off: pl.BlockSpec((None, rows, tn), lambda j, i: (layer, 0, j + off))
    e_spec = lambda off: pl.BlockSpec((tm, tn), lambda j, i: (jnp.maximum(i - n_prompt_tiles, 0), j + off))
    samp = pl.BlockSpec((tm, tn), lambda j, i: (jnp.maximum(i - n_prompt_tiles, 0), j))
    tail = pl.BlockSpec((1, SUBLANES, tn), lambda j, i: (jnp.minimum(i // tiles_per_seq, last_seq), 0, j))
    return pl.pallas_call(
        functools.partial(_ffn_up_body, tm=tm, tn=tn, csub=csub, n_prompt_tiles=n_prompt_tiles,
                          tiles_per_seq=tiles_per_seq, t_new=t_new),
        grid=(nj, m // tm),
        in_specs=[pl.BlockSpec((tm, d), lambda j, i: (i, 0)),
                  par(d, 0), par(d, nj), par(CONV_W, 0), par(CONV_W, nj), par(1, 0), par(1, nj),
                  e_spec(0), e_spec(nj)],
        out_specs=[pl.BlockSpec((tm, tn), lambda j, i: (i, j)), tail, tail, samp, samp],
        out_shape=[jax.ShapeDtypeStruct((m, f), BF16),
                   jax.ShapeDtypeStruct((n_prompt_seq, SUBLANES, f), F32),
                   jax.ShapeDtypeStruct((n_prompt_seq, SUBLANES, f), F32),
                   jax.ShapeDtypeStruct((n_sample_tiles * tm, f), F32),
                   jax.ShapeDtypeStruct((n_sample_tiles * tm, f), F32)],
        scratch_shapes=[pltpu.VMEM((SUBLANES + tm, tn), F32), pltpu.VMEM((SUBLANES + tm, tn), F32)],
        compiler_params=_cparams("parallel", "arbitrary"),
        name="ffn_up",
    )(h, w_up, w_up, w_dw, w_dw, b_dw, b_dw, e, e)


def _rope_tables(pos):
    half = MLA_ROPE // 2
    inv = ROPE_BASE ** (-jnp.arange(half, dtype=F32) / half)
    ang = pos.astype(F32)[:, None] * inv[None, :]
    cos, sin = jnp.cos(ang), jnp.sin(ang)
    zero = jnp.zeros((pos.shape[0], LANES - MLA_ROPE), F32)
    return jnp.concatenate([cos, cos, zero], axis=1), jnp.concatenate([-sin, sin, zero], axis=1)


def _pad_cols(w, width):
    return jnp.pad(w, ((0, 0), (0, width - w.shape[1])))


def _swap_halves(w):
    half = w.shape[1] // 2
    return jnp.concatenate([w[:, half:], w[:, :half]], axis=1)


def _attention_layer(x, hn, i, dims, cos_t, sin_t, tri, page_table, caches, w_in, g_q, g_kv, w_uq, w_ukv, w_out):
    n_p, t_p, n_s, t_s = dims
    mp = n_p * t_p
    pe = w_in[:, _O_PE:]
    w_ext = jnp.concatenate([w_in[:, :_O_PE], _pad_cols(pe, LANES), _pad_cols(_swap_halves(pe), LANES)],
                            axis=1).astype(BF16)
    wq = w_uq.reshape(MLA_Q_LORA, MLA_HEADS, MLA_NOPE + MLA_ROPE)
    wq_pe = wq[:, :, MLA_NOPE:]
    wq_sw = jnp.concatenate([wq_pe[..., MLA_ROPE // 2:], wq_pe[..., :MLA_ROPE // 2]], axis=-1)
    padh = lambda w: jnp.pad(w, ((0, 0), (0, 0), (0, LANES - MLA_ROPE))).reshape(MLA_Q_LORA, -1)
    wuq_ext = jnp.concatenate([wq[:, :, :MLA_NOPE].reshape(MLA_Q_LORA, -1), padh(wq_pe), padh(wq_sw)],
                              axis=1).astype(BF16)
    wuk_t = jnp.transpose(w_ukv[:, :, :MLA_NOPE], (1, 2, 0)).astype(BF16)
    wuv = jnp.transpose(w_ukv[:, :, MLA_NOPE:], (1, 0, 2)).astype(BF16)

    q_sb, k_sb, v_sb, cqn, ckv, kpe, kc = attn_in(
        hn, w_ext, g_q.reshape(1, -1), g_kv.reshape(1, -1), cos_t, sin_t)
    qcat = mla_q(cqn, wuq_ext, wuk_t, cos_t, sin_t)

    tq_sb = _pick(t_p, tri.shape[0])
    o_sb_p = sb_prompt_attention(q_sb, k_sb, v_sb, tri[:tq_sb, :tq_sb], n_p, t_p, tq_sb)
    o_mla_p = mla_prompt_attention(qcat, kc, wuv, n_p, t_p, _pick(t_p, 128), _pick(t_p, 512))

    qs = q_sb[mp:].reshape(n_s, t_s, SB_KV_HEADS, SB_REP, SB_HEAD_DIM)
    qs = jnp.transpose(qs, (0, 2, 3, 1, 4)).reshape(n_s, SB_KV_HEADS * SB_REP * t_s, SB_HEAD_DIM)
    qm = jnp.transpose(qcat[:, mp:].reshape(MLA_HEADS, n_s, t_s, KC_W), (1, 0, 2, 3))
    qm = qm.reshape(n_s, MLA_HEADS * t_s, KC_W)
    new = (k_sb[mp:].reshape(n_s, t_s * SB_KV_HEADS, SB_HEAD_DIM), v_sb[mp:].reshape(n_s, t_s * SB_KV_HEADS, SB_HEAD_DIM),
           ckv[mp:].reshape(n_s, t_s, -1), jnp.transpose(kpe[mp:].reshape(n_s, t_s, -1), (0, 2, 1)))
    n_pool, page = caches[0].shape[1:3]
    o_sb_s, o_mla_s = decode_attention(
        page_table, qs, qm[..., :MLA_KV_LORA], qm[..., MLA_KV_LORA:MLA_KV_LORA + MLA_ROPE], *new,
        wuv, tri,
        caches[0].reshape(-1, n_pool, page * SB_KV_HEADS, SB_HEAD_DIM),
        caches[1].reshape(-1, n_pool, page * SB_KV_HEADS, SB_HEAD_DIM),
        caches[2], jnp.transpose(caches[3], (0, 1, 3, 2)), i, _pick(page_table.shape[1], 16))
    o_sb_s = jnp.transpose(o_sb_s.reshape(n_s, SB_KV_HEADS, SB_REP, t_s, SB_HEAD_DIM), (0, 3, 1, 2, 4))
    o_mla_s = jnp.transpose(o_mla_s.reshape(n_s, MLA_HEADS, t_s, MLA_V), (0, 2, 1, 3))
    o = jnp.concatenate([
        jnp.concatenate([o_sb_p, o_mla_p], axis=1),
        jnp.concatenate([o_sb_s.reshape(n_s * t_s, -1), o_mla_s.reshape(n_s * t_s, -1)], axis=1)], axis=0)
    x = matmul(o, w_out, i, resid=x)
    return x, (k_sb, v_sb, ckv, kpe)


def _ssm_layer(x, hn, i, dims, h0, w_in, a_re, a_im, log_step, b_re, b_im, c_re, c_im, d_skip, w_glu, w_out):
    n_p, t_p, n_s, t_s = dims
    mp = n_p * t_p
    width = w_in.shape[2]
    g, p = a_re.shape
    gpb = 2 * LANES // SSM_GROUP
    n_blk = g // gpb
    u = matmul(hn, w_in, i)
    lam_r, lam_i, bbar_r, bbar_i = zoh(a_re, a_im, log_step, b_re, b_im)
    eye = jnp.eye(gpb, dtype=F32)

    def blockdiag_in(bb):
        bb = bb.reshape(SSM_GROUP, n_blk, gpb, p)
        return jnp.einsum('hngp,gk->nghkp', bb, eye).reshape(n_blk, gpb * SSM_GROUP, gpb * p).astype(BF16)

    def blockdiag_out(cc):
        cc = cc.reshape(n_blk, gpb, SSM_GROUP, p)
        return jnp.einsum('nghp,gk->ngpkh', cc, eye).reshape(n_blk, gpb * p, gpb * SSM_GROUP).astype(BF16)

    mats = (blockdiag_in(bbar_r), blockdiag_in(bbar_i), blockdiag_out(c_re), blockdiag_out(c_im),
            d_skip.reshape(1, width), lam_r.reshape(n_blk, 1, gpb * p), lam_i.reshape(n_blk, 1, gpb * p))
    zeros = jnp.zeros((n_p, 1, g * p), F32)
    gb = _pick(n_blk, 4)
    yp32, yp16, spr, spi = ssm_scan(u, t_p, *mats, zeros, zeros, nb=1, steps=_pick(t_p, 256), gb=gb)
    us = jnp.transpose(u[mp:].reshape(n_s, t_s, width), (1, 0, 2)).reshape(t_s * n_s, width)
    ys32, ys16, ssr, ssi = ssm_scan(us, t_s * n_s, *mats, h0[0].reshape(1, n_s, g * p),
                                    h0[1].reshape(1, n_s, g * p), nb=n_s, steps=t_s, gb=1)
    back = lambda y: jnp.transpose(y.reshape(t_s, n_s, width), (1, 0, 2)).reshape(n_s * t_s, width)
    y32 = jnp.concatenate([yp32, back(ys32)], axis=0)
    y16 = jnp.concatenate([yp16, back(ys16)], axis=0)
    gated = matmul(y16, w_glu, i, out_dtype=BF16, glu_y=y32)
    x = matmul(gated, w_out, i, resid=x)
    states = (spr.reshape(n_p, g, p), spi.reshape(n_p, g, p), ssr.reshape(n_s, g, p), ssi.reshape(n_s, g, p))
    return x, states


def _ffn_layer(x, hn, layer, dims, state, w_up, w_dw, b_dw, w_down):
    n_p, t_p, n_s, t_s = dims
    f = w_down.shape[1]
    tm = _pick(math.gcd(t_p, n_s * t_s), 512)
    e = jnp.pad(state, ((0, 0), (t_s - (CONV_W - 1), 0), (0, 0))).reshape(n_s * t_s, 2 * f)
    hid, tail_g, tail_v, up_g, up_v = ffn_up(hn, w_up, w_dw, b_dw.reshape(b_dw.shape[0], 1, 2 * f), e, layer,
                                              n_p, t_p, t_s, tm, 512)
    x = matmul(hid, w_down, layer, resid=x)
    keep = CONV_W - 1
    conv_p = jnp.concatenate([tail_g[:, SUBLANES - keep:], tail_v[:, SUBLANES - keep:]], axis=-1)
    conv_s = jnp.concatenate([up_g.reshape(n_s, t_s, f)[:, t_s - keep:], up_v.reshape(n_s, t_s, f)[:, t_s - keep:]],
                             axis=-1)
    return x, conv_p, conv_s


def kernel(x_prompt, x_sample, cache_sb_k, cache_sb_v, cache_mla_ckv, cache_mla_kpe, page_table, state_ssm_re, state_ssm_im, state_ffn_conv, norm_mix, norm_ffn, norm_final, attn_w_in, attn_g_q, attn_g_kv, attn_w_uq, attn_w_ukv, attn_w_out, ssm_w_in, ssm_a_re, ssm_a_im, ssm_log_step, ssm_b_re, ssm_b_im, ssm_c_re, ssm_c_im, ssm_d, ssm_w_glu, ssm_w_out, ffn_w_up, ffn_w_dw, ffn_b_dw, ffn_w_down):
    n_p, t_p, d = x_prompt.shape
    n_s, t_s, _ = x_sample.shape
    dims = (n_p, t_p, n_s, t_s)
    mp, ms = n_p * t_p, n_s * t_s
    depth = norm_mix.shape[0]
    past_len = page_table.shape[1] * cache_sb_k.shape[2]
    x = jnp.concatenate([x_prompt.reshape(mp, d), x_sample.reshape(ms, d)], axis=0)
    pos = jnp.concatenate([jnp.tile(jnp.arange(t_p, dtype=jnp.int32), n_p),
                           jnp.tile(past_len + jnp.arange(t_s, dtype=jnp.int32), n_s)])
    cos_t, sin_t = _rope_tables(pos)
    tri_n = 2 * LANES
    tri = (lax.broadcasted_iota(jnp.int32, (tri_n, tri_n), 0)
           > lax.broadcasted_iota(jnp.int32, (tri_n, tri_n), 1)).astype(BF16)
    caches = (cache_sb_k, cache_sb_v, cache_mla_ckv, cache_mla_kpe)
    attn_w_out16, ssm_w_in16, ssm_w_glu16, ssm_w_out16, ffn_w_up16, ffn_w_down16 = (
        w.astype(BF16) for w in (attn_w_out, ssm_w_in, ssm_w_glu, ssm_w_out, ffn_w_up, ffn_w_down))

    rows_p, rows_s, ssm_p, ssm_s, conv_p, conv_s = [], [], [], [], [], []
    for layer in range(depth):
        i = layer // 2
        hn = rmsnorm_rows(x, norm_mix[layer], BF16)
        if layer % 2 == 0:
            x, rows = _attention_layer(x, hn, i, dims, cos_t, sin_t, tri, page_table, caches,
                                       attn_w_in[i], attn_g_q[i], attn_g_kv[i], attn_w_uq[i], attn_w_ukv[i],
                                       attn_w_out16)
            rows_p.append([r[:mp] for r in rows])
            rows_s.append([r[mp:] for r in rows])
        else:
            x, st = _ssm_layer(x, hn, i, dims, (state_ssm_re[i], state_ssm_im[i]), ssm_w_in16, ssm_a_re[i],
                               ssm_a_im[i], ssm_log_step[i], ssm_b_re[i], ssm_b_im[i], ssm_c_re[i], ssm_c_im[i],
                               ssm_d[i], ssm_w_glu16, ssm_w_out16)
            ssm_p.append(st[:2])
            ssm_s.append(st[2:])
        hn = rmsnorm_rows(x, norm_ffn[layer], BF16)
        x, cp, cs = _ffn_layer(x, hn, layer, dims, state_ffn_conv[layer], ffn_w_up16, ffn_w_dw, ffn_b_dw,
                               ffn_w_down16)
        conv_p.append(cp)
        conv_s.append(cs)

    y_prompt = rmsnorm_rows(x, norm_final, F32, 0, mp).reshape(n_p, t_p, d)
    y_sample = rmsnorm_rows(x, norm_final, F32, mp, ms).reshape(n_s, t_s, d)
    kvs = (SB_KV_HEADS, SB_HEAD_DIM)
    stack = lambda rows, k, shape: jnp.stack([r[k].reshape(shape) for r in rows])
    return (y_prompt, y_sample,
            stack(rows_p, 0, (n_p, t_p) + kvs), stack(rows_p, 1, (n_p, t_p) + kvs),
            stack(rows_p, 2, (n_p, t_p, -1)), stack(rows_p, 3, (n_p, t_p, -1)),
            jnp.stack([s[0] for s in ssm_p]), jnp.stack([s[1] for s in ssm_p]), jnp.stack(conv_p),
            stack(rows_s, 0, (n_s, t_s) + kvs), stack(rows_s, 1, (n_s, t_s) + kvs),
            stack(rows_s, 2, (n_s, t_s, -1)), stack(rows_s, 3, (n_s, t_s, -1)),
            jnp.stack([s[0] for s in ssm_s]), jnp.stack([s[1] for s in ssm_s]), jnp.stack(conv_s))
```

```python
import functools
import math

import jax
import jax.numpy as jnp
from jax import lax
from jax.experimental import pallas as pl
from jax.experimental.pallas import tpu as pltpu

F32 = jnp.float32
BF16 = jnp.bfloat16
EPS = 1e-6

SB_KV_HEADS = 2
SB_REP = 4
SB_HEAD_DIM = 128
SB_SCALE = SB_HEAD_DIM ** -0.5
MLA_HEADS = 8
MLA_Q_LORA = 768
MLA_KV_LORA = 512
MLA_NOPE = 128
MLA_ROPE = 64
MLA_V = 128
MLA_SCALE = (MLA_NOPE + MLA_ROPE) ** -0.5
ROPE_BASE = 10000.0
SSM_GROUP = 16
SSM_STATE = 64
CONV_W = 3

V7X_VMEM_BYTES = 64 * 2 ** 20
VMEM_LIMIT_BYTES = V7X_VMEM_BYTES - 8 * 2 ** 20
LANES = 128
SUBLANES = 8

SB_QW = SB_KV_HEADS * SB_REP * SB_HEAD_DIM
SB_KW = SB_KV_HEADS * SB_HEAD_DIM
KC_W = MLA_KV_LORA + LANES
NEG_BIG = -1e30


def _cparams(*sem):
    return pltpu.CompilerParams(dimension_semantics=sem, vmem_limit_bytes=VMEM_LIMIT_BYTES)


def _dot(a, b):
    return jnp.dot(a, b, preferred_element_type=F32)


def _dot_nt(a, b):
    return lax.dot_general(a, b, (((1,), (1,)), ((), ())), preferred_element_type=F32)


def _pick(n, pref):
    t = min(n, pref)
    while n % t:
        t //= 2
    return t


def _rmsnorm_body(x_ref, g_ref, o_ref):
    x = x_ref[...]
    y = x * lax.rsqrt(jnp.mean(x * x, axis=-1, keepdims=True) + EPS)
    o_ref[...] = (y * g_ref[...]).astype(o_ref.dtype)


def rmsnorm_rows(x, g, out_dtype, row_start=0, n_rows=None, tm=512):
    m, d = x.shape
    n_rows = m - row_start if n_rows is None else n_rows
    tm = _pick(math.gcd(n_rows, row_start) if row_start else n_rows, tm)
    off = row_start // tm
    return pl.pallas_call(
        _rmsnorm_body,
        grid=(n_rows // tm,),
        in_specs=[pl.BlockSpec((tm, d), lambda i: (i + off, 0)),
                  pl.BlockSpec((1, d), lambda i: (0, 0))],
        out_specs=pl.BlockSpec((tm, d), lambda i: (i, 0)),
        out_shape=jax.ShapeDtypeStruct((n_rows, d), out_dtype),
        compiler_params=_cparams("parallel"),
        name="rmsnorm",
    )(x, g.reshape(1, d).astype(F32))


def _mm_body(a_ref, w_ref, o_ref):
    o_ref[...] = _dot(a_ref[...], w_ref[...]).astype(o_ref.dtype)


def _mm_resid_body(a_ref, w_ref, r_ref, o_ref):
    o_ref[...] = r_ref[...] + _dot(a_ref[...], w_ref[...])


def _glu_body(a_ref, y_ref, w_ref, o_ref):
    y = y_ref[...]
    gate = 1.0 / (1.0 + jnp.exp(-_dot(a_ref[...], w_ref[...])))
    o_ref[...] = (y * gate).astype(o_ref.dtype)


def _mm_resid_norm_body(a_ref, w_ref, r_ref, g_ref, o_ref, hn_ref):
    x = r_ref[...] + _dot(a_ref[...], w_ref[...])
    o_ref[...] = x
    y = x * lax.rsqrt(jnp.mean(x * x, axis=-1, keepdims=True) + EPS)
    hn_ref[...] = (y * g_ref[...]).astype(hn_ref.dtype)


def matmul(a, w, layer, out_dtype=F32, resid=None, glu_y=None, norm_g=None, tm=512, tn=512):
    m, k = a.shape
    n = w.shape[2]
    tm, tn = _pick(m, tm), _pick(n, tn)
    a_spec = pl.BlockSpec((tm, k), lambda i, j: (i, 0))
    w_mode = dict(pipeline_mode=pl.Buffered(1)) if tn == n else {}
    w_spec = pl.BlockSpec((None, k, tn), lambda i, j: (layer, 0, j), **w_mode)
    t_spec = pl.BlockSpec((tm, tn), lambda i, j: (i, j))
    out_specs, out_shape = t_spec, jax.ShapeDtypeStruct((m, n), out_dtype)
    if norm_g is not None:
        assert resid is not None and tn == n
        body, specs = _mm_resid_norm_body, [a_spec, w_spec, t_spec, pl.BlockSpec((1, n), lambda i, j: (0, 0))]
        args = (a, w, resid, norm_g.reshape(1, n).astype(F32))
        out_specs, out_shape = [t_spec, t_spec], [out_shape, jax.ShapeDtypeStruct((m, n), BF16)]
    elif resid is not None:
        body, specs, args = _mm_resid_body, [a_spec, w_spec, t_spec], (a, w, resid)
    elif glu_y is not None:
        body, specs, args = _glu_body, [a_spec, t_spec, w_spec], (a, glu_y, w)
    else:
        body, specs, args = _mm_body, [a_spec, w_spec], (a, w)
    return pl.pallas_call(
        body,
        grid=(m // tm, n // tn),
        in_specs=specs,
        out_specs=out_specs,
        out_shape=out_shape,
        compiler_params=_cparams("parallel", "arbitrary"),
        name="matmul",
    )(*args)


_O_K = SB_QW
_O_V = _O_K + SB_KW
_O_CQ = _O_V + SB_KW
_O_CKV = _O_CQ + MLA_Q_LORA
_O_PE = _O_CKV + MLA_KV_LORA
_O_PESW = _O_PE + LANES
ATT_IN_EXT = _O_PESW + LANES


def _attn_in_body(h_ref, w_ref, gq_ref, gkv_ref, cos_ref, sin_ref,
                  qsb_ref, k_ref, v_ref, cq_ref, ckv_ref, kpe_ref, kc_ref):
    h = h_ref[...]

    def seg(lo, hi):
        return _dot(h, w_ref[:, lo:hi])

    def norm(x, g_ref):
        return x * lax.rsqrt(jnp.mean(x * x, axis=-1, keepdims=True) + EPS) * g_ref[...]

    qsb_ref[...] = seg(0, _O_K).astype(qsb_ref.dtype)
    k_ref[...] = seg(_O_K, _O_V)
    v_ref[...] = seg(_O_V, _O_CQ)
    cq_ref[...] = norm(seg(_O_CQ, _O_CKV), gq_ref).astype(cq_ref.dtype)
    ckv = norm(seg(_O_CKV, _O_PE), gkv_ref)
    ckv_ref[...] = ckv
    kpe = seg(_O_PE, _O_PESW) * cos_ref[...] + seg(_O_PESW, ATT_IN_EXT) * sin_ref[...]
    kpe_ref[...] = kpe[:, :MLA_ROPE]
    kc_ref[:, :MLA_KV_LORA] = ckv.astype(kc_ref.dtype)
    kc_ref[:, MLA_KV_LORA:] = kpe.astype(kc_ref.dtype)


def attn_in(h, w_ext, g_q, g_kv, cos_t, sin_t, tm=256):
    m, d = h.shape
    tm = _pick(m, tm)
    row = lambda w: pl.BlockSpec((tm, w), lambda i: (i, 0))
    full = lambda a: pl.BlockSpec(a.shape, lambda i: (0,) * a.ndim)
    outs = [(SB_QW, BF16), (SB_KW, F32), (SB_KW, F32), (MLA_Q_LORA, BF16),
            (MLA_KV_LORA, F32), (MLA_ROPE, F32), (KC_W, BF16)]
    return pl.pallas_call(
        _attn_in_body,
        grid=(m // tm,),
        in_specs=[row(d), full(w_ext), full(g_q), full(g_kv), row(LANES), row(LANES)],
        out_specs=[row(w) for w, _ in outs],
        out_shape=[jax.ShapeDtypeStruct((m, w), dt) for w, dt in outs],
        compiler_params=_cparams("parallel"),
        name="attn_in",
    )(h, w_ext, g_q, g_kv, cos_t, sin_t)


def _mla_q_body(cq_ref, wuq_ref, wuk_ref, cos_ref, sin_ref, o_ref):
    cq = cq_ref[...]
    hw = MLA_HEADS * LANES
    q_nope = _dot(cq, wuq_ref[:, :hw]).astype(BF16)
    pe = _dot(cq, wuq_ref[:, hw:2 * hw])
    pe_sw = _dot(cq, wuq_ref[:, 2 * hw:])
    cos = cos_ref[...]
    sin = sin_ref[...]
    for h in range(MLA_HEADS):
        sl = slice(h * LANES, (h + 1) * LANES)
        o_ref[h, :, :MLA_KV_LORA] = _dot(q_nope[:, sl], wuk_ref[h]).astype(o_ref.dtype)
        o_ref[h, :, MLA_KV_LORA:] = (pe[:, sl] * cos + pe_sw[:, sl] * sin).astype(o_ref.dtype)


def mla_q(cqn, wuq_ext, wuk_t, cos_t, sin_t, tm=256):
    m = cqn.shape[0]
    tm = _pick(m, tm)
    row = lambda w: pl.BlockSpec((tm, w), lambda i: (i, 0))
    full = lambda a: pl.BlockSpec(a.shape, lambda i: (0,) * a.ndim)
    return pl.pallas_call(
        _mla_q_body,
        grid=(m // tm,),
        in_specs=[row(MLA_Q_LORA), full(wuq_ext), full(wuk_t), row(LANES), row(LANES)],
        out_specs=pl.BlockSpec((MLA_HEADS, tm, KC_W), lambda i: (0, i, 0)),
        out_shape=jax.ShapeDtypeStruct((MLA_HEADS, m, KC_W), BF16),
        compiler_params=_cparams("parallel"),
        name="mla_q",
    )(cqn, wuq_ext, wuk_t, cos_t, sin_t)


def _split3(x):
    hi = x.astype(BF16)
    r = x - hi.astype(F32)
    mid = r.astype(BF16)
    lo = (r - mid.astype(F32)).astype(BF16)
    return hi, mid, lo


def _later_sum(ls, tri):
    hi, mid, lo = _split3(ls)
    return _dot(hi, tri) + _dot(mid, tri) + _dot(lo, tri)


def _stick_weights(z, vis, carry, tri):
    soft = jnp.log(1.0 + jnp.exp(-jnp.abs(z)))
    log_beta = jnp.minimum(z, 0.0) - soft
    log_stay = log_beta - z
    if vis is not None:
        log_stay = jnp.where(vis, log_stay, 0.0)
    w = jnp.exp(log_beta + _later_sum(log_stay, tri) + carry)
    if vis is not None:
        w = jnp.where(vis, w, 0.0)
    return w, carry + jnp.sum(log_stay, axis=-1, keepdims=True)


def _sb_prompt_body(q_ref, k_ref, v_ref, tri_ref, o_ref, *, tq):
    qi = pl.program_id(2)
    q = q_ref[...]
    qs = jnp.concatenate([q[:, r * SB_HEAD_DIM:(r + 1) * SB_HEAD_DIM] for r in range(SB_REP)], axis=0)
    rows = SB_REP * tq
    q_pos = qi * tq + lax.broadcasted_iota(jnp.int32, (rows, tq), 0) % tq
    k_off = lax.broadcasted_iota(jnp.int32, (rows, tq), 1)
    tri = tri_ref[...]

    def step(i, state):
        acc, carry = state
        kb = qi - i
        start = pl.multiple_of(kb * tq, tq)
        k = k_ref[pl.ds(start, tq), :].astype(BF16)
        v = v_ref[pl.ds(start, tq), :].astype(BF16)
        z = _dot_nt(qs, k) * SB_SCALE
        vis = (kb * tq + k_off) < q_pos
        w, carry = _stick_weights(z, vis, carry, tri)
        return acc + _dot(w.astype(BF16), v), carry

    acc, _ = lax.fori_loop(0, qi + 1, step,
                           (jnp.zeros((rows, SB_HEAD_DIM), F32), jnp.zeros((rows, 1), F32)))
    for r in range(SB_REP):
        o_ref[:, r * SB_HEAD_DIM:(r + 1) * SB_HEAD_DIM] = acc[r * tq:(r + 1) * tq].astype(o_ref.dtype)


def sb_prompt_attention(q_sb, k_sb, v_sb, tri, n_seq, seq_len, tq):
    nq = seq_len // tq
    gw = SB_REP * SB_HEAD_DIM
    return pl.pallas_call(
        functools.partial(_sb_prompt_body, tq=tq),
        grid=(n_seq, SB_KV_HEADS, nq),
        in_specs=[pl.BlockSpec((tq, gw), lambda b, g, i: (b * nq + i, g)),
                  pl.BlockSpec((seq_len, SB_HEAD_DIM), lambda b, g, i: (b, g)),
                  pl.BlockSpec((seq_len, SB_HEAD_DIM), lambda b, g, i: (b, g)),
                  pl.BlockSpec(tri.shape, lambda b, g, i: (0, 0))],
        out_specs=pl.BlockSpec((tq, gw), lambda b, g, i: (b * nq + i, g)),
        out_shape=jax.ShapeDtypeStruct((n_seq * seq_len, SB_QW), BF16),
        compiler_params=_cparams("parallel", "parallel", "arbitrary"),
        name="sb_prompt_attention",
    )(q_sb, k_sb, v_sb, tri)


def _mla_prompt_body(q_ref, kc_ref, wuv_ref, o_ref, m_ref, l_ref, acc_ref, *, tq, tk):
    qi = pl.program_id(1)
    rows = MLA_HEADS * tq
    q = q_ref[...].reshape(rows, KC_W)
    q_pos = qi * tq + lax.broadcasted_iota(jnp.int32, (rows, tk), 0) % tq
    k_off = lax.broadcasted_iota(jnp.int32, (rows, tk), 1)
    m_ref[...] = jnp.full(m_ref.shape, NEG_BIG, F32)
    l_ref[...] = jnp.zeros(l_ref.shape, F32)
    acc_ref[...] = jnp.zeros(acc_ref.shape, F32)

    def step(kb, _):
        start = pl.multiple_of(kb * tk, tk)
        kc = kc_ref[pl.ds(start, tk), :]
        s = _dot_nt(q, kc) * MLA_SCALE
        s = jnp.where((kb * tk + k_off) <= q_pos, s, -jnp.inf)
        m_old = m_ref[...]
        m_new = jnp.maximum(m_old, jnp.max(s, axis=-1, keepdims=True))
        alpha = jnp.exp(m_old - m_new)
        p = jnp.exp(s - m_new)
        l_ref[...] = alpha * l_ref[...] + jnp.sum(p, axis=-1, keepdims=True)
        acc_ref[...] = alpha * acc_ref[...] + _dot(p.astype(BF16), kc[:, :MLA_KV_LORA])
        m_ref[...] = m_new
        return 0

    n_kb = ((qi + 1) * tq + tk - 1) // tk
    lax.fori_loop(0, n_kb, step, 0)
    o_lat = (acc_ref[...] / l_ref[...]).astype(BF16)
    for h in range(MLA_HEADS):
        o_ref[:, h * MLA_V:(h + 1) * MLA_V] = _dot(o_lat[h * tq:(h + 1) * tq], wuv_ref[h]).astype(o_ref.dtype)


def mla_prompt_attention(qcat, kc, wuv, n_seq, seq_len, tq, tk):
    nq = seq_len // tq
    rows = MLA_HEADS * tq
    return pl.pallas_call(
        functools.partial(_mla_prompt_body, tq=tq, tk=tk),
        grid=(n_seq, nq),
        in_specs=[pl.BlockSpec((MLA_HEADS, tq, KC_W), lambda b, i: (0, b * nq + i, 0)),
                  pl.BlockSpec((seq_len, KC_W), lambda b, i: (b, 0)),
                  pl.BlockSpec(wuv.shape, lambda b, i: (0, 0, 0))],
        out_specs=pl.BlockSpec((tq, MLA_HEADS * MLA_V), lambda b, i: (b * nq + i, 0)),
        out_shape=jax.ShapeDtypeStruct((n_seq * seq_len, MLA_HEADS * MLA_V), BF16),
        scratch_shapes=[pltpu.VMEM((rows, 1), F32), pltpu.VMEM((rows, 1), F32),
                        pltpu.VMEM((rows, MLA_KV_LORA), F32)],
        compiler_params=_cparams("parallel", "arbitrary"),
        name="mla_prompt_attention",
    )(qcat, kc, wuv)


def _decode_body(pt_ref, qsb_ref, qa_ref, qp_ref, nk_ref, nv_ref, nc_ref, nr_ref, wuv_ref, tri_ref,
                 ck_hbm, cv_hbm, cc_hbm, cr_hbm,
                 osb_ref, omla_ref,
                 kbuf, vbuf, cbuf, rbuf, sems, padk, padv, padc, padr,
                 sbacc_ref, carry_ref, m_ref, l_ref, acc_ref,
                 *, layer, n_seq, n_pages, ppc, page, t_new):
    b = pl.program_id(0)
    c = pl.program_id(1)
    n_chunks = n_pages // ppc
    n = b * n_chunks + c
    slot = n % 2
    hbm = (ck_hbm, cv_hbm, cc_hbm, cr_hbm)
    bufs = (kbuf, vbuf, cbuf, rbuf)
    qrows = SB_KV_HEADS * SB_REP * t_new
    hr = SB_REP * t_new
    kvrows = SB_KV_HEADS * page

    def page_copies(bb, cc, sl):
        first = bb * n_pages + (n_chunks - 1 - cc) * ppc
        out = []
        for j in range(ppc):
            pid = pt_ref[first + j]
            for a in range(4):
                out.append(pltpu.make_async_copy(hbm[a].at[layer, pid], bufs[a].at[sl, j], sems.at[sl, a]))
        return out

    @pl.when(n == 0)
    def _():
        for cp in page_copies(0, 0, 0):
            cp.start()
        for ref in (padk, padv, padc, padr):
            ref[...] = jnp.zeros(ref.shape, F32)

    @pl.when(n + 1 < n_seq * n_chunks)
    def _():
        nxt = n + 1
        for cp in page_copies(nxt // n_chunks, nxt % n_chunks, 1 - slot):
            cp.start()

    q_sb = qsb_ref[0].astype(F32)
    q_a = qa_ref[0].astype(F32)
    q_p = qp_ref[0].astype(F32)
    tri = tri_ref[...]
    own_head = (lax.broadcasted_iota(jnp.int32, (qrows, kvrows), 1) % SB_KV_HEADS
                == lax.broadcasted_iota(jnp.int32, (qrows, kvrows), 0) // hr)

    def attend(k, v, lat, rope_t, sb_vis, mla_vis):
        n_pg = len(rope_t)
        z = _dot_nt(q_sb, k) * SB_SCALE
        log_beta, stay, total = [], [], []
        for j in range(n_pg):
            zj = z[:, j * kvrows:(j + 1) * kvrows]
            soft = jnp.log(1.0 + jnp.exp(-jnp.abs(zj)))
            lb = jnp.minimum(zj, 0.0) - soft
            log_stay = jnp.where(sb_vis, lb - zj, 0.0)
            log_beta.append(lb)
            stay.append(log_stay)
            total.append(jnp.sum(log_stay, axis=-1, keepdims=True))
        n_rows = n_pg * qrows
        pieces = _split3(stay[0] if n_pg == 1 else jnp.concatenate(stay, axis=0))
        later = _dot(jnp.concatenate(pieces, axis=0), tri)
        later = later[:n_rows] + later[n_rows:2 * n_rows] + later[2 * n_rows:]
        within = [later[j * qrows:(j + 1) * qrows] for j in range(n_pg)]
        carry = carry_ref[...]
        w = [None] * n_pg
        for j in range(n_pg - 1, -1, -1):
            w[j] = jnp.where(sb_vis, jnp.exp(log_beta[j] + within[j] + carry), 0.0)
            carry = carry + total[j]
        carry_ref[...] = carry
        sbacc_ref[...] += _dot(w[0] if n_pg == 1 else jnp.concatenate(w, axis=1), v)
        s_rope = [_dot(q_p, r) for r in rope_t]
        s = (_dot_nt(q_a, lat) + (s_rope[0] if n_pg == 1 else jnp.concatenate(s_rope, axis=1))) * MLA_SCALE
        if mla_vis is not None:
            s = jnp.where(mla_vis, s, -jnp.inf)
        m_old = m_ref[...]
        m_new = jnp.maximum(m_old, jnp.max(s, axis=-1, keepdims=True))
        alpha = jnp.exp(m_old - m_new)
        p = jnp.exp(s - m_new)
        l_ref[...] = alpha * l_ref[...] + jnp.sum(p, axis=-1, keepdims=True)
        acc_ref[...] = alpha * acc_ref[...] + _dot(p, lat)
        m_ref[...] = m_new

    @pl.when(c == 0)
    def _():
        sbacc_ref[...] = jnp.zeros(sbacc_ref.shape, F32)
        carry_ref[...] = jnp.zeros(carry_ref.shape, F32)
        m_ref[...] = jnp.full(m_ref.shape, NEG_BIG, F32)
        l_ref[...] = jnp.zeros(l_ref.shape, F32)
        acc_ref[...] = jnp.zeros(acc_ref.shape, F32)
        padk[:SB_KV_HEADS * t_new] = nk_ref[0]
        padv[:SB_KV_HEADS * t_new] = nv_ref[0]
        padc[:t_new] = nc_ref[0]
        padr[:, :t_new] = nr_ref[0]
        tok_sb = lax.broadcasted_iota(jnp.int32, (qrows, kvrows), 0) % t_new
        key_sb = lax.broadcasted_iota(jnp.int32, (qrows, kvrows), 1) // SB_KV_HEADS
        tok = lax.broadcasted_iota(jnp.int32, (qrows, page), 0) % t_new
        key = lax.broadcasted_iota(jnp.int32, (qrows, page), 1)
        attend(padk[...], padv[...], padc[...], [padr[...]], own_head & (key_sb < tok_sb), key <= tok)

    for cp in page_copies(b, c, slot):
        cp.wait()

    attend(kbuf[slot].reshape(ppc * kvrows, SB_HEAD_DIM), vbuf[slot].reshape(ppc * kvrows, SB_HEAD_DIM),
           cbuf[slot].reshape(ppc * page, MLA_KV_LORA), [rbuf[slot, j] for j in range(ppc)], own_head, None)

    @pl.when(c == n_chunks - 1)
    def _():
        osb_ref[0] = sbacc_ref[...].astype(osb_ref.dtype)
        o_lat = (acc_ref[...] / l_ref[...]).astype(BF16)
        omla_ref[0] = jnp.concatenate(
            [_dot(o_lat[h * t_new:(h + 1) * t_new], wuv_ref[h]) for h in range(MLA_HEADS)],
            axis=0).astype(omla_ref.dtype)


def decode_attention(page_table, q_sb, q_a, q_p, new_k, new_v, new_c, new_r, wuv, tri,
                     cache_k, cache_v, cache_c, cache_r, layer, ppc):
    n_seq, n_pages = page_table.shape
    t_new = new_c.shape[1]
    page = cache_c.shape[2]
    kvrows = SB_KV_HEADS * page
    assert tri.shape[0] == kvrows and n_pages % ppc == 0 and t_new <= page
    assert SB_KV_HEADS * SB_REP == MLA_HEADS
    n_chunks = n_pages // ppc
    qrows = MLA_HEADS * t_new
    blk = lambda a: pl.BlockSpec((1,) + a.shape[1:], lambda b, c, pt: (b,) + (0,) * (a.ndim - 1))
    full = lambda a: pl.BlockSpec(a.shape, lambda b, c, pt: (0,) * a.ndim)
    any_spec = pl.BlockSpec(memory_space=pl.ANY)
    grid_spec = pltpu.PrefetchScalarGridSpec(
        num_scalar_prefetch=1,
        grid=(n_seq, n_chunks),
        in_specs=[blk(q_sb), blk(q_a), blk(q_p), blk(new_k), blk(new_v), blk(new_c), blk(new_r),
                  full(wuv), full(tri), any_spec, any_spec, any_spec, any_spec],
        out_specs=[pl.BlockSpec((1, qrows, SB_HEAD_DIM), lambda b, c, pt: (b, 0, 0)),
                   pl.BlockSpec((1, qrows, MLA_V), lambda b, c, pt: (b, 0, 0))],
        scratch_shapes=[pltpu.VMEM((2, ppc, kvrows, SB_HEAD_DIM), F32), pltpu.VMEM((2, ppc, kvrows, SB_HEAD_DIM), F32),
                        pltpu.VMEM((2, ppc, page, MLA_KV_LORA), F32), pltpu.VMEM((2, ppc, MLA_ROPE, page), F32),
                        pltpu.SemaphoreType.DMA((2, 4)),
                        pltpu.VMEM((kvrows, SB_HEAD_DIM), F32), pltpu.VMEM((kvrows, SB_HEAD_DIM), F32),
                        pltpu.VMEM((page, MLA_KV_LORA), F32), pltpu.VMEM((MLA_ROPE, page), F32),
                        pltpu.VMEM((qrows, SB_HEAD_DIM), F32), pltpu.VMEM((qrows, 1), F32),
                        pltpu.VMEM((qrows, 1), F32), pltpu.VMEM((qrows, 1), F32),
                        pltpu.VMEM((qrows, MLA_KV_LORA), F32)],
    )
    return pl.pallas_call(
        functools.partial(_decode_body, layer=layer, n_seq=n_seq, n_pages=n_pages, ppc=ppc,
                          page=page, t_new=t_new),
        grid_spec=grid_spec,
        out_shape=[jax.ShapeDtypeStruct((n_seq, qrows, SB_HEAD_DIM), BF16),
                   jax.ShapeDtypeStruct((n_seq, qrows, MLA_V), BF16)],
        compiler_params=_cparams("arbitrary", "arbitrary"),
        name="decode_attention",
    )(page_table.reshape(-1), q_sb, q_a, q_p, new_k, new_v, new_c, new_r, wuv, tri,
      cache_k, cache_v, cache_c, cache_r)


def _zoh_body(are_ref, aim_ref, step_ref, bre_ref, bim_ref, lr_ref, li_ref, br_ref, bi_ref):
    dt = jnp.exp(step_ref[...])
    ar = are_ref[...]
    ai = aim_ref[...]
    mag = jnp.exp(ar * dt)
    lam_r = mag * jnp.cos(ai * dt)
    lam_i = mag * jnp.sin(ai * dt)
    den = ar * ar + ai * ai
    nr = lam_r - 1.0
    fr = (nr * ar + lam_i * ai) / den
    fi = (lam_i * ar - nr * ai) / den
    lr_ref[...] = lam_r
    li_ref[...] = lam_i
    for h in range(SSM_GROUP):
        br = bre_ref[h]
        bi = bim_ref[h]
        br_ref[h] = fr * br - fi * bi
        bi_ref[h] = fr * bi + fi * br


def zoh(a_re, a_im, log_step, b_re, b_im):
    g, p = a_re.shape
    hgp = jax.ShapeDtypeStruct((SSM_GROUP, g, p), F32)
    gp = jax.ShapeDtypeStruct((g, p), F32)
    return pl.pallas_call(_zoh_body, out_shape=[gp, gp, hgp, hgp], name="ssm_zoh")(
        a_re, a_im, log_step.reshape(g, 1), jnp.transpose(b_re, (2, 0, 1)), jnp.transpose(b_im, (2, 0, 1)))


def _gelu_tanh(y):
    return 0.5 * y * (1.0 + jnp.tanh(math.sqrt(2.0 / math.pi) * (y + 0.044715 * (y * y * y))))


def _ssm_body(u_ref, bre_ref, bim_ref, cre_ref, cim_ref, d_ref, lr_ref, li_ref, h0r_ref, h0i_ref,
              y32_ref, y16_ref, sr_ref, si_ref, xr_s, xi_s, *, nb, steps, gb, cw, sw):
    tc = pl.program_id(2)
    for k in range(gb):
        u16 = u_ref[:, k * cw:(k + 1) * cw].astype(BF16)
        xr_s[k] = _dot(u16, bre_ref[k])
        xi_s[k] = _dot(u16, bim_ref[k])
    lam = [(lr_ref[k], li_ref[k]) for k in range(gb)]

    @pl.when(tc == 0)
    def _():
        sr_ref[0] = h0r_ref[0]
        si_ref[0] = h0i_ref[0]

    def step(t, state):
        rows = pl.ds(pl.multiple_of(t * nb, nb), nb)
        out = []
        for k in range(gb):
            sr, si = state[2 * k], state[2 * k + 1]
            lam_r, lam_i = lam[k]
            nr = lam_r * sr - lam_i * si + xr_s[k, rows, :]
            ni = lam_r * si + lam_i * sr + xi_s[k, rows, :]
            xr_s[k, rows, :] = nr
            xi_s[k, rows, :] = ni
            out += [nr, ni]
        return tuple(out)

    init = []
    for k in range(gb):
        init += [sr_ref[0, :, k * sw:(k + 1) * sw], si_ref[0, :, k * sw:(k + 1) * sw]]
    final = lax.fori_loop(0, steps, step, tuple(init))
    for k in range(gb):
        sr_ref[0, :, k * sw:(k + 1) * sw] = final[2 * k]
        si_ref[0, :, k * sw:(k + 1) * sw] = final[2 * k + 1]
        cs = slice(k * cw, (k + 1) * cw)
        y = _dot(xr_s[k].astype(BF16), cre_ref[k]) - _dot(xi_s[k].astype(BF16), cim_ref[k])
        y = _gelu_tanh(y + d_ref[:, cs] * u_ref[:, cs])
        y32_ref[:, cs] = y
        y16_ref[:, cs] = y.astype(BF16)


def ssm_scan(u, group_rows, bre, bim, cre, cim, d, lam_r, lam_i, h0r, h0i, nb, steps, gb):
    w = u.shape[1]
    n_blk, cw, sw = bre.shape
    n_grp = h0r.shape[0]
    tr = steps * nb
    assert n_blk % gb == 0 and group_rows % tr == 0
    n_t = group_rows // tr
    rows = n_grp * group_rows
    act = pl.BlockSpec((tr, gb * cw), lambda s, j, t: (s * n_t + t, j))
    par = lambda a, b: pl.BlockSpec((gb, a, b), lambda s, j, t: (j, 0, 0))
    st = pl.BlockSpec((1, nb, gb * sw), lambda s, j, t: (s, 0, j))
    return pl.pallas_call(
        functools.partial(_ssm_body, nb=nb, steps=steps, gb=gb, cw=cw, sw=sw),
        grid=(n_grp, n_blk // gb, n_t),
        in_specs=[act, par(cw, sw), par(cw, sw), par(sw, cw), par(sw, cw),
                  pl.BlockSpec((1, gb * cw), lambda s, j, t: (0, j)), par(1, sw), par(1, sw), st, st],
        out_specs=[act, act, st, st],
        out_shape=[jax.ShapeDtypeStruct((rows, w), F32), jax.ShapeDtypeStruct((rows, w), BF16),
                   jax.ShapeDtypeStruct(h0r.shape, F32), jax.ShapeDtypeStruct(h0i.shape, F32)],
        scratch_shapes=[pltpu.VMEM((gb, tr, sw), F32), pltpu.VMEM((gb, tr, sw), F32)],
        compiler_params=_cparams("parallel", "parallel", "arbitrary"),
        name="ssm_scan",
    )(u, bre, bim, cre, cim, d, lam_r, lam_i, h0r, h0i)


def _ffn_up_body(h_ref, wg_ref, wv_ref, dwg_ref, dwv_ref, bg_ref, bv_ref, eg_ref, ev_ref,
                 hid_ref, tailg_ref, tailv_ref, upg_ref, upv_ref, ug_ref, uv_ref,
                 *, tm, tn, csub, n_prompt_tiles, tiles_per_seq, t_new):
    i = pl.program_id(1)
    h = h_ref[...]
    lo = SUBLANES
    sides = ((wg_ref, ug_ref, dwg_ref, bg_ref, eg_ref), (wv_ref, uv_ref, dwv_ref, bv_ref, ev_ref))

    def tile(sample):
        rt = lax.broadcasted_iota(jnp.int32, (tm, 1), 0) % t_new
        for c0 in range(0, tn, csub):
            cs = slice(c0, c0 + csub)
            mixed = []
            for w_ref, u_ref, dw_ref, b_ref, e_ref in sides:
                u_ref[lo:, cs] = _dot(h, w_ref[:, cs])
                m1 = u_ref[lo - 1:lo - 1 + tm, cs]
                m2 = u_ref[lo - 2:lo - 2 + tm, cs]
                if sample:
                    e = e_ref[:, cs]
                    m1 = jnp.where(rt >= 1, m1, pltpu.roll(e, tm - (t_new - 1), 0))
                    m2 = jnp.where(rt >= 2, m2, pltpu.roll(e, tm - (t_new - 2), 0))
                mixed.append(((b_ref[:, cs] + dw_ref[0:1, cs] * m2) + dw_ref[1:2, cs] * m1)
                             + dw_ref[2:3, cs] * u_ref[lo:, cs])
            gate, val = mixed
            hid_ref[:, cs] = (gate * (1.0 / (1.0 + jnp.exp(-gate))) * val).astype(hid_ref.dtype)

    @pl.when(i < n_prompt_tiles)
    def _():
        @pl.when(i % tiles_per_seq == 0)
        def _():
            ug_ref[:lo] = jnp.zeros((lo, tn), F32)
            uv_ref[:lo] = jnp.zeros((lo, tn), F32)

        tile(False)
        for u_ref, tail_ref in ((ug_ref, tailg_ref), (uv_ref, tailv_ref)):
            last = u_ref[tm:]
            tail_ref[0] = last
            u_ref[:lo] = last

    @pl.when(i >= n_prompt_tiles)
    def _():
        tile(True)
        upg_ref[...] = ug_ref[lo:]
        upv_ref[...] = uv_ref[lo:]


def ffn_up(h, w_up, w_dw, b_dw, e, layer, n_prompt_seq, seq_len, t_new, tm, tn):
    m, d = h.shape
    f = w_up.shape[2] // 2
    n_prompt_tiles = n_prompt_seq * seq_len // tm
    tiles_per_seq = seq_len // tm
    ms = m - n_prompt_seq * seq_len
    n_sample_tiles = ms // tm
    assert seq_len % tm == 0 and ms % tm == 0 and tm % t_new == 0 and t_new >= CONV_W - 1
    assert n_prompt_tiles > 0 and CONV_W - 1 <= SUBLANES
    tn = _pick(f, tn)
    csub = _pick(tn, 2 * LANES)
    nj = f // tn
    last_seq = n_prompt_seq - 1
    par = lambda rows, off: pl.BlockSpec((None, rows, tn), lambda j, i: (layer, 0, j + off))
    e_spec = lambda off: pl.BlockSpec((tm, tn), lambda j, i: (jnp.maximum(i - n_prompt_tiles, 0), j + off))
    samp = pl.BlockSpec((tm, tn), lambda j, i: (jnp.maximum(i - n_prompt_tiles, 0), j))
    tail = pl.BlockSpec((1, SUBLANES, tn), lambda j, i: (jnp.minimum(i // tiles_per_seq, last_seq), 0, j))
    return pl.pallas_call(
        functools.partial(_ffn_up_body, tm=tm, tn=tn, csub=csub, n_prompt_tiles=n_prompt_tiles,
                          tiles_per_seq=tiles_per_seq, t_new=t_new),
        grid=(nj, m // tm),
        in_specs=[pl.BlockSpec((tm, d), lambda j, i: (i, 0)),
                  par(d, 0), par(d, nj), par(CONV_W, 0), par(CONV_W, nj), par(1, 0), par(1, nj),
                  e_spec(0), e_spec(nj)],
        out_specs=[pl.BlockSpec((tm, tn), lambda j, i: (i, j)), tail, tail, samp, samp],
        out_shape=[jax.ShapeDtypeStruct((m, f), BF16),
                   jax.ShapeDtypeStruct((n_prompt_seq, SUBLANES, f), F32),
                   jax.ShapeDtypeStruct((n_prompt_seq, SUBLANES, f), F32),
                   jax.ShapeDtypeStruct((n_sample_tiles * tm, f), F32),
                   jax.ShapeDtypeStruct((n_sample_tiles * tm, f), F32)],
        scratch_shapes=[pltpu.VMEM((SUBLANES + tm, tn), F32), pltpu.VMEM((SUBLANES + tm, tn), F32)],
        compiler_params=_cparams("parallel", "arbitrary"),
        name="ffn_up",
    )(h, w_up, w_up, w_dw, w_dw, b_dw, b_dw, e, e)


def _rope_tables(pos):
    half = MLA_ROPE // 2
    inv = ROPE_BASE ** (-jnp.arange(half, dtype=F32) / half)
    ang = pos.astype(F32)[:, None] * inv[None, :]
    cos, sin = jnp.cos(ang), jnp.sin(ang)
    zero = jnp.zeros((pos.shape[0], LANES - MLA_ROPE), F32)
    return jnp.concatenate([cos, cos, zero], axis=1), jnp.concatenate([-sin, sin, zero], axis=1)


def _pad_cols(w, width):
    return jnp.pad(w, ((0, 0), (0, width - w.shape[1])))


def _swap_halves(w):
    half = w.shape[1] // 2
    return jnp.concatenate([w[:, half:], w[:, :half]], axis=1)


def _attention_layer(x, hn, i, dims, cos_t, sin_t, tri, page_table, caches, w_in, g_q, g_kv, w_uq, w_ukv, w_out,
                     g_next):
    n_p, t_p, n_s, t_s = dims
    mp = n_p * t_p
    pe = w_in[:, _O_PE:]
    w_ext = jnp.concatenate([w_in[:, :_O_PE], _pad_cols(pe, LANES), _pad_cols(_swap_halves(pe), LANES)],
                            axis=1).astype(BF16)
    wq = w_uq.reshape(MLA_Q_LORA, MLA_HEADS, MLA_NOPE + MLA_ROPE)
    wq_pe = wq[:, :, MLA_NOPE:]
    wq_sw = jnp.concatenate([wq_pe[..., MLA_ROPE // 2:], wq_pe[..., :MLA_ROPE // 2]], axis=-1)
    padh = lambda w: jnp.pad(w, ((0, 0), (0, 0), (0, LANES - MLA_ROPE))).reshape(MLA_Q_LORA, -1)
    wuq_ext = jnp.concatenate([wq[:, :, :MLA_NOPE].reshape(MLA_Q_LORA, -1), padh(wq_pe), padh(wq_sw)],
                              axis=1).astype(BF16)
    wuk_t = jnp.transpose(w_ukv[:, :, :MLA_NOPE], (1, 2, 0)).astype(BF16)
    wuv = jnp.transpose(w_ukv[:, :, MLA_NOPE:], (1, 0, 2)).astype(BF16)

    q_sb, k_sb, v_sb, cqn, ckv, kpe, kc = attn_in(
        hn, w_ext, g_q.reshape(1, -1), g_kv.reshape(1, -1), cos_t, sin_t)
    qcat = mla_q(cqn, wuq_ext, wuk_t, cos_t, sin_t)

    tq_sb = _pick(t_p, tri.shape[0])
    o_sb_p = sb_prompt_attention(q_sb, k_sb, v_sb, tri[:tq_sb, :tq_sb], n_p, t_p, tq_sb)
    o_mla_p = mla_prompt_attention(qcat, kc, wuv, n_p, t_p, _pick(t_p, 128), _pick(t_p, 512))

    qs = q_sb[mp:].reshape(n_s, t_s, SB_KV_HEADS, SB_REP, SB_HEAD_DIM)
    qs = jnp.transpose(qs, (0, 2, 3, 1, 4)).reshape(n_s, SB_KV_HEADS * SB_REP * t_s, SB_HEAD_DIM)
    qm = jnp.transpose(qcat[:, mp:].reshape(MLA_HEADS, n_s, t_s, KC_W), (1, 0, 2, 3))
    qm = qm.reshape(n_s, MLA_HEADS * t_s, KC_W)
    new = (k_sb[mp:].reshape(n_s, t_s * SB_KV_HEADS, SB_HEAD_DIM), v_sb[mp:].reshape(n_s, t_s * SB_KV_HEADS, SB_HEAD_DIM),
           ckv[mp:].reshape(n_s, t_s, -1), jnp.transpose(kpe[mp:].reshape(n_s, t_s, -1), (0, 2, 1)))
    n_pool, page = caches[0].shape[1:3]
    o_sb_s, o_mla_s = decode_attention(
        page_table, qs, qm[..., :MLA_KV_LORA], qm[..., MLA_KV_LORA:MLA_KV_LORA + MLA_ROPE], *new,
        wuv, tri,
        caches[0].reshape(-1, n_pool, page * SB_KV_HEADS, SB_HEAD_DIM),
        caches[1].reshape(-1, n_pool, page * SB_KV_HEADS, SB_HEAD_DIM),
        caches[2], jnp.transpose(caches[3], (0, 1, 3, 2)), i, _pick(page_table.shape[1], 16))
    o_sb_s = jnp.transpose(o_sb_s.reshape(n_s, SB_KV_HEADS, SB_REP, t_s, SB_HEAD_DIM), (0, 3, 1, 2, 4))
    o_mla_s = jnp.transpose(o_mla_s.reshape(n_s, MLA_HEADS, t_s, MLA_V), (0, 2, 1, 3))
    o = jnp.concatenate([
        jnp.concatenate([o_sb_p, o_mla_p], axis=1),
        jnp.concatenate([o_sb_s.reshape(n_s * t_s, -1), o_mla_s.reshape(n_s * t_s, -1)], axis=1)], axis=0)
    x, hn_next = matmul(o, w_out, i, resid=x, norm_g=g_next, tm=256, tn=w_out.shape[2])
    return x, hn_next, (k_sb, v_sb, ckv, kpe)


def _ssm_layer(x, hn, i, dims, h0, w_in, a_re, a_im, log_step, b_re, b_im, c_re, c_im, d_skip, w_glu, w_out,
               g_next):
    n_p, t_p, n_s, t_s = dims
    mp = n_p * t_p
    width = w_in.shape[2]
    g, p = a_re.shape
    gpb = 2 * LANES // SSM_GROUP
    n_blk = g // gpb
    u = matmul(hn, w_in, i, tn=width)
    lam_r, lam_i, bbar_r, bbar_i = zoh(a_re, a_im, log_step, b_re, b_im)
    eye = jnp.eye(gpb, dtype=F32)

    def blockdiag_in(bb):
        bb = bb.reshape(SSM_GROUP, n_blk, gpb, p)
        return jnp.einsum('hngp,gk->nghkp', bb, eye).reshape(n_blk, gpb * SSM_GROUP, gpb * p).astype(BF16)

    def blockdiag_out(cc):
        cc = cc.reshape(n_blk, gpb, SSM_GROUP, p)
        return jnp.einsum('nghp,gk->ngpkh', cc, eye).reshape(n_blk, gpb * p, gpb * SSM_GROUP).astype(BF16)

    mats = (blockdiag_in(bbar_r), blockdiag_in(bbar_i), blockdiag_out(c_re), blockdiag_out(c_im),
            d_skip.reshape(1, width), lam_r.reshape(n_blk, 1, gpb * p), lam_i.reshape(n_blk, 1, gpb * p))
    zeros = jnp.zeros((n_p, 1, g * p), F32)
    gb = _pick(n_blk, 4)
    yp32, yp16, spr, spi = ssm_scan(u, t_p, *mats, zeros, zeros, nb=1, steps=_pick(t_p, 256), gb=gb)
    us = jnp.transpose(u[mp:].reshape(n_s, t_s, width), (1, 0, 2)).reshape(t_s * n_s, width)
    ys32, ys16, ssr, ssi = ssm_scan(us, t_s * n_s, *mats, h0[0].reshape(1, n_s, g * p),
                                    h0[1].reshape(1, n_s, g * p), nb=n_s, steps=t_s, gb=1)
    back = lambda y: jnp.transpose(y.reshape(t_s, n_s, width), (1, 0, 2)).reshape(n_s * t_s, width)
    y32 = jnp.concatenate([yp32, back(ys32)], axis=0)
    y16 = jnp.concatenate([yp16, back(ys16)], axis=0)
    gated = matmul(y16, w_glu, i, out_dtype=BF16, glu_y=y32, tn=width)
    x, hn_next = matmul(gated, w_out, i, resid=x, norm_g=g_next, tm=256, tn=w_out.shape[2])
    states = (spr.reshape(n_p, g, p), spi.reshape(n_p, g, p), ssr.reshape(n_s, g, p), ssi.reshape(n_s, g, p))
    return x, hn_next, states


def _ffn_layer(x, hn, layer, dims, state, w_up, w_dw, b_dw, w_down, g_next):
    n_p, t_p, n_s, t_s = dims
    f = w_down.shape[1]
    tm = _pick(math.gcd(t_p, n_s * t_s), 1024)
    e = jnp.pad(state, ((0, 0), (t_s - (CONV_W - 1), 0), (0, 0))).reshape(n_s * t_s, 2 * f)
    hid, tail_g, tail_v, up_g, up_v = ffn_up(hn, w_up, w_dw, b_dw.reshape(b_dw.shape[0], 1, 2 * f), e, layer,
                                              n_p, t_p, t_s, tm, 512)
    if g_next is None:
        x, hn_next = matmul(hid, w_down, layer, resid=x, tm=256, tn=w_down.shape[2]), None
    else:
        x, hn_next = matmul(hid, w_down, layer, resid=x, norm_g=g_next, tm=256, tn=w_down.shape[2])
    keep = CONV_W - 1
    conv_p = jnp.concatenate([tail_g[:, SUBLANES - keep:], tail_v[:, SUBLANES - keep:]], axis=-1)
    conv_s = jnp.concatenate([up_g.reshape(n_s, t_s, f)[:, t_s - keep:], up_v.reshape(n_s, t_s, f)[:, t_s - keep:]],
                             axis=-1)
    return x, hn_next, conv_p, conv_s


def kernel(x_prompt, x_sample, cache_sb_k, cache_sb_v, cache_mla_ckv, cache_mla_kpe, page_table, state_ssm_re, state_ssm_im, state_ffn_conv, norm_mix, norm_ffn, norm_final, attn_w_in, attn_g_q, attn_g_kv, attn_w_uq, attn_w_ukv, attn_w_out, ssm_w_in, ssm_a_re, ssm_a_im, ssm_log_step, ssm_b_re, ssm_b_im, ssm_c_re, ssm_c_im, ssm_d, ssm_w_glu, ssm_w_out, ffn_w_up, ffn_w_dw, ffn_b_dw, ffn_w_down):
    n_p, t_p, d = x_prompt.shape
    n_s, t_s, _ = x_sample.shape
    dims = (n_p, t_p, n_s, t_s)
    mp, ms = n_p * t_p, n_s * t_s
    depth = norm_mix.shape[0]
    past_len = page_table.shape[1] * cache_sb_k.shape[2]
    x = jnp.concatenate([x_prompt.reshape(mp, d), x_sample.reshape(ms, d)], axis=0)
    pos = jnp.concatenate([jnp.tile(jnp.arange(t_p, dtype=jnp.int32), n_p),
                           jnp.tile(past_len + jnp.arange(t_s, dtype=jnp.int32), n_s)])
    cos_t, sin_t = _rope_tables(pos)
    tri_n = 2 * LANES
    tri = (lax.broadcasted_iota(jnp.int32, (tri_n, tri_n), 0)
           > lax.broadcasted_iota(jnp.int32, (tri_n, tri_n), 1)).astype(BF16)
    caches = (cache_sb_k, cache_sb_v, cache_mla_ckv, cache_mla_kpe)
    attn_w_out16, ssm_w_in16, ssm_w_glu16, ssm_w_out16, ffn_w_up16, ffn_w_down16 = (
        w.astype(BF16) for w in (attn_w_out, ssm_w_in, ssm_w_glu, ssm_w_out, ffn_w_up, ffn_w_down))

    rows_p, rows_s, ssm_p, ssm_s, conv_p, conv_s = [], [], [], [], [], []
    hn = rmsnorm_rows(x, norm_mix[0], BF16)
    for layer in range(depth):
        i = layer // 2
        if layer % 2 == 0:
            x, hn, rows = _attention_layer(x, hn, i, dims, cos_t, sin_t, tri, page_table, caches,
                                           attn_w_in[i], attn_g_q[i], attn_g_kv[i], attn_w_uq[i], attn_w_ukv[i],
                                           attn_w_out16, norm_ffn[layer])
            rows_p.append([r[:mp] for r in rows])
            rows_s.append([r[mp:] for r in rows])
        else:
            x, hn, st = _ssm_layer(x, hn, i, dims, (state_ssm_re[i], state_ssm_im[i]), ssm_w_in16, ssm_a_re[i],
                                   ssm_a_im[i], ssm_log_step[i], ssm_b_re[i], ssm_b_im[i], ssm_c_re[i],
                                   ssm_c_im[i], ssm_d[i], ssm_w_glu16, ssm_w_out16, norm_ffn[layer])
            ssm_p.append(st[:2])
            ssm_s.append(st[2:])
        x, hn, cp, cs = _ffn_layer(x, hn, layer, dims, state_ffn_conv[layer], ffn_w_up16, ffn_w_dw, ffn_b_dw,
                                   ffn_w_down16, norm_mix[layer + 1] if layer + 1 < depth else None)
        conv_p.append(cp)
        conv_s.append(cs)

    y_prompt = rmsnorm_rows(x, norm_final, F32, 0, mp).reshape(n_p, t_p, d)
    y_sample = rmsnorm_rows(x, norm_final, F32, mp, ms).reshape(n_s, t_s, d)
    kvs = (SB_KV_HEADS, SB_HEAD_DIM)
    stack = lambda rows, k, shape: jnp.stack([r[k].reshape(shape) for r in rows])
    return (y_prompt, y_sample,
            stack(rows_p, 0, (n_p, t_p) + kvs), stack(rows_p, 1, (n_p, t_p) + kvs),
            stack(rows_p, 2, (n_p, t_p, -1)), stack(rows_p, 3, (n_p, t_p, -1)),
            jnp.stack([s[0] for s in ssm_p]), jnp.stack([s[1] for s in ssm_p]), jnp.stack(conv_p),
            stack(rows_s, 0, (n_s, t_s) + kvs), stack(rows_s, 1, (n_s, t_s) + kvs),
            stack(rows_s, 2, (n_s, t_s, -1)), stack(rows_s, 3, (n_s, t_s, -1)),
            jnp.stack([s[0] for s in ssm_s]), jnp.stack([s[1] for s in ssm_s]), jnp.stack(conv_s))
```
